```python
import jax
import jax.numpy as jnp
from jax import lax
import numpy as np

D_MODEL = 2048
BATCH = 4
SEQ = 2048
DEPTH = 4

CHUNK = 64
Q_BLOCK = 128
ATTN_WIDTH = D_MODEL // 2
ATTN_HEAD_DIM = 128
ATTN_HEADS = ATTN_WIDTH // ATTN_HEAD_DIM
CONV_CH = D_MODEL // 4
DW_CONV_LEN = 31
LRU_WIDTH = D_MODEL // 4
LRU_BLOCKS = 4
LRU_BLOCK_DIM = LRU_WIDTH // LRU_BLOCKS
LRU_CONV_LEN = 4
LRU_C = 8.0
MIX_WIDTH = ATTN_WIDTH + CONV_CH + LRU_WIDTH
IN_COLS = 3 * ATTN_WIDTH + 2 * CONV_CH + 2 * LRU_WIDTH
SPLITS = [ATTN_WIDTH, 2 * ATTN_WIDTH, 3 * ATTN_WIDTH,
          3 * ATTN_WIDTH + CONV_CH, 3 * ATTN_WIDTH + 2 * CONV_CH,
          3 * ATTN_WIDTH + 2 * CONV_CH + LRU_WIDTH]
D_FF = -(-8 * D_MODEL // (3 * 256)) * 256
EPS = 1e-6

kernel_name = "hybrid_stickbreak_conformer_rglru_block"


def rms_norm(x, g):
    xf = x.astype(jnp.float32)
    y = xf * lax.rsqrt(jnp.mean(xf * xf, axis=-1, keepdims=True) + EPS)
    return (y * g.astype(jnp.float32)).astype(x.dtype)


def layer_norm(x, g, b):
    xf = x.astype(jnp.float32)
    mu = jnp.mean(xf, axis=-1, keepdims=True)
    xc = xf - mu
    y = xc * lax.rsqrt(jnp.mean(xc * xc, axis=-1, keepdims=True) + EPS)
    return (y * g.astype(jnp.float32) + b.astype(jnp.float32)).astype(x.dtype)


def causal_depthwise_conv(x, w, b):
    width, ch = w.shape
    y = lax.conv_general_dilated(x, w[:, None, :].astype(x.dtype), window_strides=(1,),
                                 padding=[(width - 1, 0)],
                                 dimension_numbers=('NWC', 'WIO', 'NWC'),
                                 feature_group_count=ch)
    return y + b.astype(x.dtype)


def stick_breaking_attention(q, k, v):
    _, s_len, _, dh = q.shape
    scale = dh ** -0.5
    outs = []
    for blk in range(s_len // Q_BLOCK):
        q0 = blk * Q_BLOCK
        kv_len = q0 + Q_BLOCK
        qb = q[:, q0:kv_len].astype(jnp.float32)
        kb = k[:, :kv_len].astype(jnp.float32)
        vb = v[:, :kv_len].astype(jnp.float32)
        z = jnp.einsum('bqhd,bkhd->bhqk', qb, kb) * scale
        q_pos = q0 + jnp.arange(Q_BLOCK)
        k_pos = jnp.arange(kv_len)
        mask = k_pos[None, :] < q_pos[:, None]
        log_keep = jnp.where(mask, jax.nn.log_sigmoid(-z), 0.0)
        log_later = lax.cumsum(log_keep, axis=3, reverse=True) - log_keep
        log_w = jax.nn.log_sigmoid(z) + log_later
        w = jnp.where(mask, jnp.exp(log_w), 0.0)
        outs.append(jnp.einsum('bhqk,bkhd->bqhd', w, vb))
    return jnp.concatenate(outs, axis=1).astype(v.dtype)


def conformer_conv(c_val, c_gate, dw_w, dw_b, ln_g, ln_b):
    u = c_val * jax.nn.sigmoid(c_gate)
    u = causal_depthwise_conv(u, dw_w, dw_b)
    u = layer_norm(u, ln_g, ln_b)
    return jax.nn.silu(u)


def griffin_recurrent(r_x, r_y, conv_w, conv_b, w_a, b_a, w_i, b_i, lam):
    xr = causal_depthwise_conv(r_x, conv_w, conv_b)
    bsz, s_len, width = xr.shape
    xb = xr.reshape(bsz, s_len, LRU_BLOCKS, LRU_BLOCK_DIM)
    gate_a = jnp.einsum('bsni,nio->bsno', xb, w_a).reshape(bsz, s_len, width) + b_a
    gate_i = jnp.einsum('bsni,nio->bsno', xb, w_i).reshape(bsz, s_len, width) + b_i
    r = jax.nn.sigmoid(gate_a.astype(jnp.float32))
    i = jax.nn.sigmoid(gate_i.astype(jnp.float32))
    log_a = -LRU_C * r * jax.nn.softplus(-lam.astype(jnp.float32))
    a = jnp.exp(log_a)
    b = jnp.sqrt(-jnp.expm1(2.0 * log_a)) * (i * xr.astype(jnp.float32))

    def combine(left, right):
        a1, b1 = left
        a2, b2 = right
        return a1 * a2, a2 * b1 + b2

    _, h = lax.associative_scan(combine, (a, b), axis=1)
    return (h * jax.nn.gelu(r_y.astype(jnp.float32))).astype(r_x.dtype)


def setup_inputs(seed: int = 0) -> dict:
    key = jax.random.key(seed)
    ks = jax.random.split(key, 25)
    f32 = jnp.float32

    def nrm(k, shape, scale):
        return jax.random.normal(k, shape, f32) * scale

    def gain(k, shape):
        return 1.0 + 0.02 * jax.random.normal(k, shape, f32)

    u = jax.random.uniform(ks[20], (DEPTH, LRU_WIDTH), f32, 0.9, 0.999)
    a_base = u ** (1.0 / LRU_C)
    lam = jnp.log(a_base) - jnp.log1p(-a_base)
    return {
        'x': jax.random.normal(ks[0], (BATCH, SEQ, D_MODEL), f32),
        'w_in': nrm(ks[1], (DEPTH, D_MODEL, IN_COLS), D_MODEL ** -0.5),
        'w_out': nrm(ks[2], (DEPTH, MIX_WIDTH, D_MODEL), MIX_WIDTH ** -0.5),
        'g_pre_mix': gain(ks[3], (DEPTH, D_MODEL)),
        'g_post_mix': gain(ks[4], (DEPTH, D_MODEL)),
        'g_pre_ffn': gain(ks[5], (DEPTH, D_MODEL)),
        'g_post_ffn': gain(ks[6], (DEPTH, D_MODEL)),
        'g_attn_grp': gain(ks[7], (DEPTH, ATTN_WIDTH)),
        'g_conv_grp': gain(ks[8], (DEPTH, CONV_CH)),
        'g_lru_grp': gain(ks[9], (DEPTH, LRU_WIDTH)),
        'dw_conv_w': nrm(ks[10], (DEPTH, DW_CONV_LEN, CONV_CH), DW_CONV_LEN ** -0.5),
        'dw_conv_b': nrm(ks[11], (DEPTH, CONV_CH), 0.01),
        'conv_ln_g': gain(ks[12], (DEPTH, CONV_CH)),
        'conv_ln_b': nrm(ks[13], (DEPTH, CONV_CH), 0.01),
        'lru_conv_w': nrm(ks[14], (DEPTH, LRU_CONV_LEN, LRU_WIDTH), LRU_CONV_LEN ** -0.5),
        'lru_conv_b': nrm(ks[15], (DEPTH, LRU_WIDTH), 0.01),
        'lru_w_a': nrm(ks[16], (DEPTH, LRU_BLOCKS, LRU_BLOCK_DIM, LRU_BLOCK_DIM), LRU_BLOCK_DIM ** -0.5),
        'lru_b_a': nrm(ks[17], (DEPTH, LRU_WIDTH), 0.01),
        'lru_w_i': nrm(ks[18], (DEPTH, LRU_BLOCKS, LRU_BLOCK_DIM, LRU_BLOCK_DIM), LRU_BLOCK_DIM ** -0.5),
        'lru_b_i': nrm(ks[19], (DEPTH, LRU_WIDTH), 0.01),
        'lru_lambda': lam,
        'w_gate': nrm(ks[21], (DEPTH, D_MODEL, D_FF), D_MODEL ** -0.5),
        'w_up': nrm(ks[22], (DEPTH, D_MODEL, D_FF), D_MODEL ** -0.5),
        'w_down': nrm(ks[23], (DEPTH, D_FF, D_MODEL), D_FF ** -0.5),
    }


def reference(x, w_in, w_out, g_pre_mix, g_post_mix, g_pre_ffn, g_post_ffn,
              g_attn_grp, g_conv_grp, g_lru_grp, dw_conv_w, dw_conv_b, conv_ln_g, conv_ln_b,
              lru_conv_w, lru_conv_b, lru_w_a, lru_b_a, lru_w_i, lru_b_i, lru_lambda,
              w_gate, w_up, w_down):
    bsz, s_len, _ = x.shape
    heads = (bsz, s_len, ATTN_HEADS, ATTN_HEAD_DIM)
    h = x
    for l in range(DEPTH):
        u = rms_norm(h, g_pre_mix[l])
        proj = jnp.einsum('bsd,dc->bsc', u, w_in[l])
        q, k, v, c_val, c_gate, r_x, r_y = jnp.split(proj, SPLITS, axis=-1)
        y_attn = stick_breaking_attention(q.reshape(heads), k.reshape(heads),
                                          v.reshape(heads)).reshape(bsz, s_len, ATTN_WIDTH)
        y_conv = conformer_conv(c_val, c_gate, dw_conv_w[l], dw_conv_b[l],
                                conv_ln_g[l], conv_ln_b[l])
        y_lru = griffin_recurrent(r_x, r_y, lru_conv_w[l], lru_conv_b[l], lru_w_a[l], lru_b_a[l],
                                  lru_w_i[l], lru_b_i[l], lru_lambda[l])
        mixed = jnp.concatenate([rms_norm(y_attn, g_attn_grp[l]),
                                 rms_norm(y_conv, g_conv_grp[l]),
                                 rms_norm(y_lru, g_lru_grp[l])], axis=-1)
        h = h + rms_norm(jnp.einsum('bsc,cd->bsd', mixed, w_out[l]), g_post_mix[l])
        u = rms_norm(h, g_pre_ffn[l])
        f = jax.nn.silu(jnp.einsum('bsd,df->bsf', u, w_gate[l])) * jnp.einsum('bsd,df->bsf', u, w_up[l])
        h = h + rms_norm(jnp.einsum('bsf,fd->bsd', f, w_down[l]), g_post_ffn[l])
    return h
```

```python
import functools

import jax
import jax.numpy as jnp
from jax import lax
from jax.experimental import pallas as pl
from jax.experimental.pallas import tpu as pltpu

F32 = jnp.float32
BF16 = jnp.bfloat16

EPS = 1e-6
HEAD_DIM = 128
DW_CONV_LEN = 31
LRU_CONV_LEN = 4
LRU_BLOCKS = 4
LRU_C = 8.0

V7X_VMEM_BYTES = 64 * 1024 * 1024
VMEM_CAP_BYTES = V7X_VMEM_BYTES - 8 * 1024 * 1024
SUBLANES = 8

LOG_WEIGHT_FLOOR = -104.0

CONV_HALO = 32
LRU_HALO = 8


def _nbytes(shape, dtype):
    n = 1
    for s in shape:
        n *= s
    return n * jnp.dtype(dtype).itemsize


def _vmem_limit(blocks, scratch=(), temps=()):
    total = 2 * sum(_nbytes(s, d) for s, d in blocks)
    total += sum(_nbytes(s, d) for s, d in scratch)
    total += sum(_nbytes(s, d) for s, d in temps)
    assert total <= VMEM_CAP_BYTES, total
    return total


def _rms(x, g):
    ms = jnp.mean(x * x, axis=-1, keepdims=True)
    return x * lax.rsqrt(ms + EPS) * g


def _softplus(z):
    return jnp.maximum(z, 0.0) + jnp.log1p(jnp.exp(-jnp.abs(z)))


def _prenorm_kernel(x_ref, g_ref, o_ref):
    o_ref[...] = _rms(x_ref[...], g_ref[...]).astype(o_ref.dtype)


def _prenorm(h, g, layer, *, tm=512):
    m, d = h.shape
    return pl.pallas_call(
        _prenorm_kernel,
        grid=(m // tm,),
        in_specs=[pl.BlockSpec((tm, d), lambda i: (i, 0)),
                  pl.BlockSpec((None, 1, d), lambda i: (layer, 0, 0))],
        out_specs=pl.BlockSpec((tm, d), lambda i: (i, 0)),
        out_shape=jax.ShapeDtypeStruct((m, d), BF16),
        compiler_params=pltpu.CompilerParams(
            dimension_semantics=("parallel",),
            vmem_limit_bytes=_vmem_limit([((tm, d), F32), ((tm, d), BF16)],
                                         temps=[((tm, d), F32)] * 2)),
        name="prenorm",
    )(h, g)


def _matmul_kernel(a_ref, w_ref, o_ref):
    o_ref[...] = jnp.dot(a_ref[...], w_ref[...],
                         preferred_element_type=F32).astype(o_ref.dtype)


def _in_proj(u, w, layer, *, col_block0, n_out, out_dtype, tm=1024, tn=1024):
    m, k = u.shape
    return pl.pallas_call(
        _matmul_kernel,
        grid=(m // tm, n_out // tn),
        in_specs=[pl.BlockSpec((tm, k), lambda i, j: (i, 0)),
                  pl.BlockSpec((None, k, tn), lambda i, j: (layer, 0, j + col_block0))],
        out_specs=pl.BlockSpec((tm, tn), lambda i, j: (i, j)),
        out_shape=jax.ShapeDtypeStruct((m, n_out), out_dtype),
        compiler_params=pltpu.CompilerParams(
            dimension_semantics=("parallel", "arbitrary"),
            vmem_limit_bytes=_vmem_limit(
                [((tm, k), BF16), ((k, tn), BF16), ((tm, tn), out_dtype)],
                temps=[((tm, tn), F32)])),
        name="in_proj",
    )(u, w)


def _attn_kernel(q_ref, k_ref, v_ref, tri_ref, o_ref, acc_ref, carry_ref, *, tq, tk):
    qi = pl.program_id(2)
    q = q_ref[...]
    tri = tri_ref[...]
    scale = HEAD_DIM ** -0.5
    acc_ref[...] = jnp.zeros_like(acc_ref)
    carry_ref[...] = jnp.zeros_like(carry_ref)
    n_diag = tq // tk

    def sweep(k_start, masked):
        kb = k_ref[pl.ds(k_start, tk), :]
        vb = v_ref[pl.ds(k_start, tk), :]
        z = lax.dot_general(q, kb, (((1,), (1,)), ((), ())),
                            preferred_element_type=F32) * scale
        log_keep = -_softplus(z)
        if masked:
            q_pos = qi * tq + lax.broadcasted_iota(jnp.int32, (tq, tk), 0)
            k_pos = k_start + lax.broadcasted_iota(jnp.int32, (tq, tk), 1)
            valid = k_pos < q_pos
            log_keep = jnp.where(valid, log_keep, 0.0)
        hi = log_keep.astype(BF16)
        lo = (log_keep - hi.astype(F32)).astype(BF16)
        sums = (jnp.dot(hi, tri, preferred_element_type=F32)
                + jnp.dot(lo, tri, preferred_element_type=F32))
        later = sums[:, :tk]
        total = sums[:, tk:]
        carry = carry_ref[...]
        log_w = z + log_keep + later + carry
        if masked:
            log_w = jnp.where(valid, log_w, -jnp.inf)
        w = jnp.exp(log_w)
        acc_ref[...] += jnp.dot(w.astype(BF16), vb, preferred_element_type=F32)
        carry_ref[...] = carry + total

    for d in range(n_diag):
        sweep((qi * n_diag + (n_diag - 1 - d)) * tk, True)

    def cond(c):
        kb_idx, carry_max = c
        return jnp.logical_and(kb_idx >= 0, carry_max > LOG_WEIGHT_FLOOR)

    def body(c):
        kb_idx, _ = c
        sweep(pl.multiple_of(kb_idx * tk, tk), False)
        return kb_idx - 1, jnp.max(carry_ref[...])

    lax.while_loop(cond, body, (qi * n_diag - 1, jnp.max(carry_ref[...])))
    o_ref[...] = acc_ref[...]


def _attention(qkv, tri, *, heads, tq=128, tk=128):
    b, s, _ = qkv.shape
    dh = HEAD_DIM
    return pl.pallas_call(
        functools.partial(_attn_kernel, tq=tq, tk=tk),
        grid=(b, heads, s // tq),
        in_specs=[pl.BlockSpec((None, tq, dh), lambda bi, h, i: (bi, i, h)),
                  pl.BlockSpec((None, s, dh), lambda bi, h, i: (bi, 0, heads + h)),
                  pl.BlockSpec((None, s, dh), lambda bi, h, i: (bi, 0, 2 * heads + h)),
                  pl.BlockSpec((tk, 2 * tk), lambda bi, h, i: (0, 0))],
        out_specs=pl.BlockSpec((None, tq, dh), lambda bi, h, i: (bi, i, h)),
        out_shape=jax.ShapeDtypeStruct((b, s, heads * dh), F32),
        scratch_shapes=[pltpu.VMEM((tq, dh), F32), pltpu.VMEM((tq, tk), F32)],
        compiler_params=pltpu.CompilerParams(
            dimension_semantics=("parallel", "parallel", "arbitrary"),
            vmem_limit_bytes=_vmem_limit(
                [((tq, dh), BF16), ((s, dh), BF16), ((s, dh), BF16),
                 ((tk, 2 * tk), BF16), ((tq, dh), F32)],
                scratch=[((tq, dh), F32), ((tq, tk), F32)],
                temps=[((tq, 2 * tk), F32)] * 8)),
        name="stickbreak_attn",
    )(qkv, qkv, qkv, tri)


def _mixer_kernel(cv_ref, cg_ref, rx_ref, ry_ref, dww_ref, pv_ref, wai_ref, o_ref,
                  ubuf, rxbuf, hcarry, *, ts, ch):
    s = pl.program_id(1)

    @pl.when(s == 0)
    def _():
        ubuf[0:CONV_HALO, :] = jnp.zeros((CONV_HALO, ch), F32)
        rxbuf[0:LRU_HALO, :] = jnp.zeros((LRU_HALO, ch), F32)
        hcarry[...] = jnp.zeros_like(hcarry)

    ubuf[CONV_HALO:CONV_HALO + ts, :] = cv_ref[...] * jax.nn.sigmoid(cg_ref[...])
    base = CONV_HALO - (DW_CONV_LEN - 1)
    acc = jnp.broadcast_to(pv_ref[0:1, :], (ts, ch))
    for k in range(DW_CONV_LEN):
        acc = acc + dww_ref[k:k + 1, :] * ubuf[base + k:base + k + ts, :]
    ubuf[0:CONV_HALO, :] = ubuf[ts:ts + CONV_HALO, :]
    mu = jnp.mean(acc, axis=-1, keepdims=True)
    xc = acc - mu
    var = jnp.mean(xc * xc, axis=-1, keepdims=True)
    y = xc * lax.rsqrt(var + EPS) * pv_ref[1:2, :] + pv_ref[2:3, :]
    y = y * jax.nn.sigmoid(y)
    o_ref[:, 0:ch] = _rms(y, pv_ref[3:4, :]).astype(o_ref.dtype)

    rxbuf[LRU_HALO:LRU_HALO + ts, :] = rx_ref[...]
    base = LRU_HALO - (LRU_CONV_LEN - 1)
    xr = jnp.broadcast_to(pv_ref[4:5, :], (ts, ch))
    for k in range(LRU_CONV_LEN):
        xr = xr + pv_ref[9 + k:10 + k, :] * rxbuf[base + k:base + k + ts, :]
    rxbuf[0:LRU_HALO, :] = rxbuf[ts:ts + LRU_HALO, :]
    xb = xr.astype(BF16)
    bd = ch // LRU_BLOCKS
    gates = [jnp.dot(xb[:, n * bd:(n + 1) * bd], wai_ref[n], preferred_element_type=F32)
             for n in range(LRU_BLOCKS)]
    gate_a = jnp.concatenate([g[:, :bd] for g in gates], axis=1) + pv_ref[5:6, :]
    gate_i = jnp.concatenate([g[:, bd:] for g in gates], axis=1) + pv_ref[6:7, :]
    r = jax.nn.sigmoid(gate_a)
    i = jax.nn.sigmoid(gate_i)
    log_a = -LRU_C * r * _softplus(-pv_ref[7:8, :])
    a = jnp.exp(log_a)
    b = jnp.sqrt(-jnp.tanh(log_a) * (a * a + 1.0)) * (i * xr)
    row = lax.broadcasted_iota(jnp.int32, (ts, ch), 0)
    d = 1
    while d < ts:
        a_prev = jnp.where(row >= d, pltpu.roll(a, d, 0), 1.0)
        b_prev = jnp.where(row >= d, pltpu.roll(b, d, 0), 0.0)
        b = a * b_prev + b
        a = a * a_prev
        d *= 2
    h = b + a * hcarry[0:1, :]
    hcarry[...] = jnp.broadcast_to(h[ts - 1:ts, :], hcarry.shape)
    y = h * jax.nn.gelu(ry_ref[...])
    o_ref[:, ch:2 * ch] = _rms(y, pv_ref[8:9, :]).astype(o_ref.dtype)


def _mixers(rest, dww, pv, wai, layer, *, ts=256):
    b, s, four_ch = rest.shape
    ch = four_ch // 4
    bd = ch // LRU_BLOCKS
    blk = lambda c: pl.BlockSpec((None, ts, ch), lambda bi, si: (bi, si, c))
    return pl.pallas_call(
        functools.partial(_mixer_kernel, ts=ts, ch=ch),
        grid=(b, s // ts),
        in_specs=[blk(0), blk(1), blk(2), blk(3),
                  pl.BlockSpec((None,) + dww.shape[1:], lambda bi, si: (layer, 0, 0)),
                  pl.BlockSpec((None,) + pv.shape[1:], lambda bi, si: (layer, 0, 0)),
                  pl.BlockSpec((None, LRU_BLOCKS, bd, 2 * bd), lambda bi, si: (layer, 0, 0, 0))],
        out_specs=pl.BlockSpec((None, ts, 2 * ch), lambda bi, si: (bi, si, 0)),
        out_shape=jax.ShapeDtypeStruct((b, s, 2 * ch), BF16),
        scratch_shapes=[pltpu.VMEM((CONV_HALO + ts, ch), F32),
                        pltpu.VMEM((LRU_HALO + ts, ch), F32),
                        pltpu.VMEM((SUBLANES, ch), F32)],
        compiler_params=pltpu.CompilerParams(
            dimension_semantics=("parallel", "arbitrary"),
            vmem_limit_bytes=_vmem_limit(
                [((ts, ch), F32)] * 4 + [(dww.shape[1:], F32), (pv.shape[1:], F32),
                                         ((LRU_BLOCKS, bd, 2 * bd), BF16), ((ts, 2 * ch), BF16)],
                scratch=[((CONV_HALO + ts, ch), F32), ((LRU_HALO + ts, ch), F32)],
                temps=[((ts, ch), F32)] * 16)),
        name="seq_mixers",
    )(rest, rest, rest, rest, dww, pv, wai)


def _out_proj_kernel(ya_ref, mr_ref, h_ref, w1_ref, w2_ref, ga_ref, gpost_ref, gffn_ref,
                     hout_ref, u_ref):
    na = _rms(ya_ref[...], ga_ref[...]).astype(BF16)
    o = (jnp.dot(na, w1_ref[...], preferred_element_type=F32)
         + jnp.dot(mr_ref[...], w2_ref[...], preferred_element_type=F32))
    hn = h_ref[...] + _rms(o, gpost_ref[...])
    hout_ref[...] = hn
    u_ref[...] = _rms(hn, gffn_ref[...]).astype(u_ref.dtype)


def _out_proj(ya, mr, h, w_out, g_attn, g_post, g_ffn, layer, *, tm=256):
    m, d = h.shape
    ka = ya.shape[1]
    kr = mr.shape[1]
    assert ka == kr and ka + kr == w_out.shape[1]
    gain = lambda n: pl.BlockSpec((None, 1, n), lambda i: (layer, 0, 0))
    return pl.pallas_call(
        _out_proj_kernel,
        grid=(m // tm,),
        in_specs=[pl.BlockSpec((tm, ka), lambda i: (i, 0)),
                  pl.BlockSpec((tm, kr), lambda i: (i, 0)),
                  pl.BlockSpec((tm, d), lambda i: (i, 0)),
                  pl.BlockSpec((None, ka, d), lambda i: (layer, 0, 0)),
                  pl.BlockSpec((None, kr, d), lambda i: (layer, 1, 0)),
                  gain(ka), gain(d), gain(d)],
        out_specs=[pl.BlockSpec((tm, d), lambda i: (i, 0)),
                   pl.BlockSpec((tm, d), lambda i: (i, 0))],
        out_shape=[jax.ShapeDtypeStruct((m, d), F32), jax.ShapeDtypeStruct((m, d), BF16)],
        compiler_params=pltpu.CompilerParams(
            dimension_semantics=("parallel",),
            vmem_limit_bytes=_vmem_limit(
                [((tm, ka), F32), ((tm, kr), BF16), ((tm, d), F32), ((ka, d), BF16),
                 ((kr, d), BF16), ((tm, d), F32), ((tm, d), BF16)],
                temps=[((tm, d), F32)] * 4)),
        name="out_proj",
    )(ya, mr, h, w_out, w_out, g_attn, g_post, g_ffn)


def _ffn_kernel(u_ref, wg_ref, wu_ref, wd_ref, h_ref, gpost_ref, gnext_ref, *rest, emit_next):
    if emit_next:
        hout_ref, unext_ref, acc_ref = rest
    else:
        hout_ref, acc_ref = rest
    f = pl.program_id(1)

    @pl.when(f == 0)
    def _():
        acc_ref[...] = jnp.zeros_like(acc_ref)

    u = u_ref[...]
    g = jnp.dot(u, wg_ref[...], preferred_element_type=F32)
    up = jnp.dot(u, wu_ref[...], preferred_element_type=F32)
    act = (g * jax.nn.sigmoid(g) * up).astype(BF16)
    acc_ref[...] += jnp.dot(act, wd_ref[...], preferred_element_type=F32)

    @pl.when(f == pl.num_programs(1) - 1)
    def _():
        hn = h_ref[...] + _rms(acc_ref[...], gpost_ref[...])
        hout_ref[...] = hn
        if emit_next:
            unext_ref[...] = _rms(hn, gnext_ref[...]).astype(unext_ref.dtype)


def _ffn(u, h, wg, wu, wd, g_post, g_next, layer, next_layer, *, emit_next, tm=512, tf=512):
    m, d = h.shape
    dff = wg.shape[2]
    out_specs = [pl.BlockSpec((tm, d), lambda i, f: (i, 0))]
    out_shape = [jax.ShapeDtypeStruct((m, d), F32)]
    blocks = [((tm, d), BF16), ((d, tf), BF16), ((d, tf), BF16), ((tf, d), BF16),
              ((tm, d), F32), ((tm, d), F32)]
    if emit_next:
        out_specs.append(pl.BlockSpec((tm, d), lambda i, f: (i, 0)))
        out_shape.append(jax.ShapeDtypeStruct((m, d), BF16))
        blocks.append(((tm, d), BF16))
    return pl.pallas_call(
        functools.partial(_ffn_kernel, emit_next=emit_next),
        grid=(m // tm, dff // tf),
        in_specs=[pl.BlockSpec((tm, d), lambda i, f: (i, 0)),
                  pl.BlockSpec((None, d, tf), lambda i, f: (layer, 0, f)),
                  pl.BlockSpec((None, d, tf), lambda i, f: (layer, 0, f)),
                  pl.BlockSpec((None, tf, d), lambda i, f: (layer, f, 0)),
                  pl.BlockSpec((tm, d), lambda i, f: (i, 0)),
                  pl.BlockSpec((None, 1, d), lambda i, f: (layer, 0, 0)),
                  pl.BlockSpec((None, 1, d), lambda i, f: (next_layer, 0, 0))],
        out_specs=out_specs,
        out_shape=out_shape,
        scratch_shapes=[pltpu.VMEM((tm, d), F32)],
        compiler_params=pltpu.CompilerParams(
            dimension_semantics=("parallel", "arbitrary"),
            vmem_limit_bytes=_vmem_limit(blocks, scratch=[((tm, d), F32)],
                                         temps=[((tm, tf), F32)] * 3 + [((tm, d), F32)] * 2)),
        name="swiglu_ffn",
    )(u, wg, wu, wd, h, g_post, g_next)


def kernel(x, w_in, w_out, g_pre_mix, g_post_mix, g_pre_ffn, g_post_ffn, g_attn_grp,
           g_conv_grp, g_lru_grp, dw_conv_w, dw_conv_b, conv_ln_g, conv_ln_b, lru_conv_w,
           lru_conv_b, lru_w_a, lru_b_a, lru_w_i, lru_b_i, lru_lambda, w_gate, w_up, w_down):
    bsz, s_len, d = x.shape
    depth = w_in.shape[0]
    attn_w = g_attn_grp.shape[1]
    ch = g_conv_grp.shape[1]
    heads = attn_w // HEAD_DIM
    m = bsz * s_len
    assert w_in.shape[2] == 3 * attn_w + 4 * ch and lru_w_a.shape[1] == LRU_BLOCKS

    w_in_b = w_in.astype(BF16)
    w_out_b = w_out.astype(BF16)
    w_gate_b = w_gate.astype(BF16)
    w_up_b = w_up.astype(BF16)
    w_down_b = w_down.astype(BF16)
    wai = jnp.concatenate([lru_w_a, lru_w_i], axis=-1).astype(BF16)

    row3 = lambda v: v[:, None, :]
    pv = jnp.concatenate(
        [row3(dw_conv_b), row3(conv_ln_g), row3(conv_ln_b), row3(g_conv_grp), row3(lru_conv_b),
         row3(lru_b_a), row3(lru_b_i), row3(lru_lambda), row3(g_lru_grp), lru_conv_w,
         jnp.zeros((depth, 3, ch), F32)], axis=1)
    dww = jnp.concatenate([dw_conv_w, jnp.zeros((depth, 1, ch), F32)], axis=1)
    g_pre_mix3, g_post_mix3 = row3(g_pre_mix), row3(g_post_mix)
    g_pre_ffn3, g_post_ffn3 = row3(g_pre_ffn), row3(g_post_ffn)
    g_attn3 = row3(g_attn_grp)

    tk = 128
    jj = lax.broadcasted_iota(jnp.int32, (tk, tk), 0)
    ss = lax.broadcasted_iota(jnp.int32, (tk, tk), 1)
    tri = jnp.concatenate([(jj > ss).astype(BF16), jnp.ones((tk, tk), BF16)], axis=1)

    tn = 1024
    h = x.reshape(m, d)
    u = _prenorm(h, g_pre_mix3, 0)
    for l in range(depth):
        qkv = _in_proj(u, w_in_b, l, col_block0=0, n_out=3 * attn_w, out_dtype=BF16, tn=tn)
        rest = _in_proj(u, w_in_b, l, col_block0=3 * attn_w // tn, n_out=4 * ch,
                        out_dtype=F32, tn=tn)
        ya = _attention(qkv.reshape(bsz, s_len, 3 * attn_w), tri, heads=heads, tk=tk)
        mr = _mixers(rest.reshape(bsz, s_len, 4 * ch), dww, pv, wai, l)
        h, u2 = _out_proj(ya.reshape(m, attn_w), mr.reshape(m, 2 * ch), h, w_out_b,
                          g_attn3, g_post_mix3, g_pre_ffn3, l)
        last = l == depth - 1
        outs = _ffn(u2, h, w_gate_b, w_up_b, w_down_b, g_post_ffn3, g_pre_mix3, l,
                    0 if last else l + 1, emit_next=not last)
        if last:
            (h,) = outs
        else:
            h, u = outs
    return h.reshape(bsz, s_len, d)
```

```python
import functools

import jax
import jax.numpy as jnp
from jax import lax
from jax.experimental import pallas as pl
from jax.experimental.pallas import tpu as pltpu

F32 = jnp.float32
BF16 = jnp.bfloat16

EPS = 1e-6
HEAD_DIM = 128
DW_CONV_LEN = 31
LRU_CONV_LEN = 4
LRU_BLOCKS = 4
LRU_C = 8.0

V7X_VMEM_BYTES = 64 * 1024 * 1024
VMEM_CAP_BYTES = V7X_VMEM_BYTES - 8 * 1024 * 1024
SUBLANES = 8
LANES = 128

LOG_WEIGHT_FLOOR = -104.0

ATTN_BLK = LANES
ATTN_FIRST_BLOCKS = 3

CONV_HALO = 32
LRU_HALO = 8
CONV_ROWS = 64


def _nbytes(shape, dtype):
    n = 1
    for s in shape:
        n *= s
    return n * jnp.dtype(dtype).itemsize


def _vmem_limit(blocks, scratch=(), temps=()):
    total = 2 * sum(_nbytes(s, d) for s, d in blocks)
    total += sum(_nbytes(s, d) for s, d in scratch)
    total += sum(_nbytes(s, d) for s, d in temps)
    assert total <= VMEM_CAP_BYTES, total
    return total


def _rms(x, g):
    ms = jnp.mean(x * x, axis=-1, keepdims=True)
    return x * lax.rsqrt(ms + EPS) * g


def _softplus(z):
    return jnp.maximum(z, 0.0) + jnp.log(1.0 + jnp.exp(-jnp.abs(z)))


def _prenorm_kernel(x_ref, g_ref, o_ref):
    o_ref[...] = _rms(x_ref[...], g_ref[...]).astype(o_ref.dtype)


def _prenorm(h, g, layer, *, tm=512):
    m, d = h.shape
    return pl.pallas_call(
        _prenorm_kernel,
        grid=(m // tm,),
        in_specs=[pl.BlockSpec((tm, d), lambda i: (i, 0)),
                  pl.BlockSpec((None, 1, d), lambda i: (layer, 0, 0))],
        out_specs=pl.BlockSpec((tm, d), lambda i: (i, 0)),
        out_shape=jax.ShapeDtypeStruct((m, d), BF16),
        compiler_params=pltpu.CompilerParams(
            dimension_semantics=("parallel",),
            vmem_limit_bytes=_vmem_limit([((tm, d), F32), ((tm, d), BF16)],
                                         temps=[((tm, d), F32)] * 2)),
        name="prenorm",
    )(h, g)


def _matmul_kernel(a_ref, w_ref, o_ref):
    o_ref[...] = jnp.dot(a_ref[...], w_ref[...],
                         preferred_element_type=F32).astype(o_ref.dtype)


def _in_proj(u, w, layer, *, col_block0, n_out, out_dtype, tm=1024, tn=1024):
    m, k = u.shape
    return pl.pallas_call(
        _matmul_kernel,
        grid=(m // tm, n_out // tn),
        in_specs=[pl.BlockSpec((tm, k), lambda i, j: (i, 0)),
                  pl.BlockSpec((None, k, tn), lambda i, j: (layer, 0, j + col_block0))],
        out_specs=pl.BlockSpec((tm, tn), lambda i, j: (i, j)),
        out_shape=jax.ShapeDtypeStruct((m, n_out), out_dtype),
        compiler_params=pltpu.CompilerParams(
            dimension_semantics=("parallel", "arbitrary"),
            vmem_limit_bytes=_vmem_limit(
                [((tm, k), BF16), ((k, tn), BF16), ((tm, tn), out_dtype)],
                temps=[((tm, tn), F32)])),
        name="in_proj",
    )(u, w)


def _attn_kernel(q_ref, k_ref, v_ref, tri_ref, o_ref, carry_ref, *, units):
    tb = ATTN_BLK
    qi = pl.program_id(2)
    q0 = qi * (units * tb)
    tri = tri_ref[...]
    scale = HEAD_DIM ** -0.5
    tt = lax.broadcasted_iota(jnp.int32, (tb, tb), 0)
    ss = lax.broadcasted_iota(jnp.int32, (tb, tb), 1)
    below_diag = ss < tt

    def last_block(x, nblk, fn):
        head, tail = x[:, :(nblk - 1) * tb], fn(x[:, (nblk - 1) * tb:])
        return tail if nblk == 1 else jnp.concatenate([head, tail], axis=1)

    def sweep(starts, blocks, *, diag, carries=None, exists=None):
        rng = range(units)
        kbs = [k_ref[pl.ds(starts[r], blocks[r] * tb), :] for r in rng]
        vbs = [v_ref[pl.ds(starts[r], blocks[r] * tb), :] for r in rng]
        zs = [lax.dot_general(q_ref[r * tb:(r + 1) * tb, :], kbs[r], (((1,), (1,)), ((), ())),
                              preferred_element_type=F32) * scale for r in rng]
        log_keeps, pieces = [], []
        for r in rng:
            nblk = blocks[r]
            log_keep = -_softplus(zs[r])
            if diag:
                log_keep = last_block(log_keep, nblk, lambda x: jnp.where(below_diag, x, 0.0))
            if exists is not None:
                log_keep = jnp.where(exists[r], log_keep, 0.0)
            hi = log_keep.astype(BF16)
            lo = (log_keep - hi.astype(F32)).astype(BF16)
            log_keeps.append(log_keep)
            pieces.append(jnp.concatenate(
                [jnp.concatenate([hi[:, j * tb:(j + 1) * tb], lo[:, j * tb:(j + 1) * tb]], axis=1)
                 for j in range(nblk)], axis=0))
        sums = [jnp.dot(pieces[r], tri, preferred_element_type=F32) for r in rng]
        ws, new_carries = [], []
        for r in rng:
            nblk = blocks[r]
            running = None if carries is None else carries[r]
            later = [None] * nblk
            for j in reversed(range(nblk)):
                local = sums[r][j * tb:(j + 1) * tb, :tb]
                total = sums[r][j * tb:(j + 1) * tb, tb:]
                later[j] = local if running is None else local + running
                running = total if running is None else running + total
            log_w = zs[r] + log_keeps[r] + (later[0] if nblk == 1
                                            else jnp.concatenate(later, axis=1))
            if diag:
                log_w = last_block(log_w, nblk, lambda x: jnp.where(below_diag, x, -jnp.inf))
            w = jnp.exp(log_w)
            if exists is not None:
                w = jnp.where(exists[r], w, 0.0)
            ws.append(w.astype(BF16))
            new_carries.append(running)
        outs = [jnp.dot(ws[r], vbs[r], preferred_element_type=F32) for r in rng]
        for r in rng:
            rows = slice(r * tb, (r + 1) * tb)
            o_ref[rows, :] = outs[r] if carries is None else o_ref[rows, :] + outs[r]
            carry_ref[r] = new_carries[r]

    @pl.when(qi == 0)
    def _():
        blocks = [min(r + 1, ATTN_FIRST_BLOCKS) for r in range(units)]
        sweep([(r + 1 - n) * tb for r, n in enumerate(blocks)], blocks, diag=True)

    @pl.when(qi > 0)
    def _():
        sweep([pl.multiple_of(q0 + (r + 1 - ATTN_FIRST_BLOCKS) * tb, tb) for r in range(units)],
              [ATTN_FIRST_BLOCKS] * units, diag=True)

    def carry_max():
        m = carry_ref[0]
        for r in range(1, units):
            m = jnp.maximum(m, carry_ref[r])
        return jnp.max(m)

    def next_start(r, it):
        return q0 + (r - ATTN_FIRST_BLOCKS - it) * tb

    def cond(c):
        it, cmax = c
        return jnp.logical_and(next_start(units - 1, it) >= 0, cmax > LOG_WEIGHT_FLOOR)

    def body(c):
        it, _ = c
        starts = [next_start(r, it) for r in range(units)]
        sweep([pl.multiple_of(jnp.maximum(k_lo, 0), tb) for k_lo in starts], [1] * units,
              diag=False, carries=[carry_ref[r] for r in range(units)],
              exists=[k_lo >= 0 for k_lo in starts])
        return it + 1, carry_max()

    lax.while_loop(cond, body, (jnp.int32(0), carry_max()))


def _attention(qkv, tri, *, heads, units=4):
    b, s, _ = qkv.shape
    dh = HEAD_DIM
    tq = units * ATTN_BLK
    return pl.pallas_call(
        functools.partial(_attn_kernel, units=units),
        grid=(b, heads, s // tq),
        in_specs=[pl.BlockSpec((None, tq, dh), lambda bi, h, i: (bi, i, h)),
                  pl.BlockSpec((None, s, dh), lambda bi, h, i: (bi, 0, heads + h)),
                  pl.BlockSpec((None, s, dh), lambda bi, h, i: (bi, 0, 2 * heads + h)),
                  pl.BlockSpec(tri.shape, lambda bi, h, i: (0, 0))],
        out_specs=pl.BlockSpec((None, tq, dh), lambda bi, h, i: (bi, i, h)),
        out_shape=jax.ShapeDtypeStruct((b, s, heads * dh), F32),
        scratch_shapes=[pltpu.VMEM((units, ATTN_BLK, ATTN_BLK), F32)],
        compiler_params=pltpu.CompilerParams(
            dimension_semantics=("parallel", "parallel", "parallel"),
            vmem_limit_bytes=_vmem_limit(
                [((tq, dh), BF16), ((s, dh), BF16), ((s, dh), BF16),
                 (tri.shape, BF16), ((tq, dh), F32)],
                scratch=[((units, ATTN_BLK, ATTN_BLK), F32)],
                temps=[((ATTN_BLK, ATTN_FIRST_BLOCKS * ATTN_BLK), F32)] * 8 * units)),
        name="stickbreak_attn",
    )(qkv, qkv, qkv, tri)


def _mixer_kernel(cv_ref, cg_ref, rx_ref, ry_ref, dww_ref, pv_ref, wai_ref, o_ref,
                  ush, rxbuf, hcarry, *, ts, ch):
    s = pl.program_id(1)

    @pl.when(s == 0)
    def _():
        ush[0, 0:CONV_HALO, :] = jnp.zeros((CONV_HALO, ch), F32)
        rxbuf[0:LRU_HALO, :] = jnp.zeros((LRU_HALO, ch), F32)
        hcarry[...] = jnp.zeros_like(hcarry)

    ush[0, CONV_HALO:CONV_HALO + ts, :] = cv_ref[...] * jax.nn.sigmoid(cg_ref[...])
    u_all = ush[0]
    for p in range(1, SUBLANES):
        ush[p] = pltpu.roll(u_all, p, 0)
    base = CONV_HALO - (DW_CONV_LEN - 1)
    for rt in range(ts // CONV_ROWS):
        r0 = rt * CONV_ROWS
        acc = jnp.broadcast_to(pv_ref[0:1, :], (CONV_ROWS, ch))
        for k in range(DW_CONV_LEN):
            p = -(base + k) % SUBLANES
            lo = base + k + p + r0
            acc = acc + dww_ref[k:k + 1, :] * ush[p, lo:lo + CONV_ROWS, :]
        mu = jnp.mean(acc, axis=-1, keepdims=True)
        xc = acc - mu
        var = jnp.mean(xc * xc, axis=-1, keepdims=True)
        y = xc * lax.rsqrt(var + EPS) * pv_ref[1:2, :] + pv_ref[2:3, :]
        y = y * jax.nn.sigmoid(y)
        o_ref[r0:r0 + CONV_ROWS, 0:ch] = _rms(y, pv_ref[3:4, :]).astype(o_ref.dtype)
    ush[0, 0:CONV_HALO, :] = ush[0, ts:ts + CONV_HALO, :]

    rxbuf[LRU_HALO:LRU_HALO + ts, :] = rx_ref[...]
    base = LRU_HALO - (LRU_CONV_LEN - 1)
    xr = jnp.broadcast_to(pv_ref[4:5, :], (ts, ch))
    for k in range(LRU_CONV_LEN):
        xr = xr + pv_ref[9 + k:10 + k, :] * rxbuf[base + k:base + k + ts, :]
    rxbuf[0:LRU_HALO, :] = rxbuf[ts:ts + LRU_HALO, :]
    xb = xr.astype(BF16)
    bd = ch // LRU_BLOCKS
    gates = [jnp.dot(xb[:, n * bd:(n + 1) * bd], wai_ref[n], preferred_element_type=F32)
             for n in range(LRU_BLOCKS)]
    gate_a = jnp.concatenate([g[:, :bd] for g in gates], axis=1) + pv_ref[5:6, :]
    gate_i = jnp.concatenate([g[:, bd:] for g in gates], axis=1) + pv_ref[6:7, :]
    r = jax.nn.sigmoid(gate_a)
    i = jax.nn.sigmoid(gate_i)
    log_a = -LRU_C * r * _softplus(-pv_ref[7:8, :])
    a = jnp.exp(log_a)
    b = jnp.sqrt(-jnp.tanh(log_a) * (a * a + 1.0)) * (i * xr)
    row = lax.broadcasted_iota(jnp.int32, (ts, ch), 0)
    d = 1
    while d < ts:
        a_prev = jnp.where(row >= d, pltpu.roll(a, d, 0), 1.0)
        b_prev = jnp.where(row >= d, pltpu.roll(b, d, 0), 0.0)
        b = a * b_prev + b
        a = a * a_prev
        d *= 2
    h = b + a * hcarry[0:1, :]
    hcarry[...] = jnp.broadcast_to(h[ts - 1:ts, :], hcarry.shape)
    y = h * jax.nn.gelu(ry_ref[...])
    o_ref[:, ch:2 * ch] = _rms(y, pv_ref[8:9, :]).astype(o_ref.dtype)


def _mixers(rest, dww, pv, wai, layer, *, ts=256):
    b, s, four_ch = rest.shape
    ch = four_ch // 4
    bd = ch // LRU_BLOCKS
    assert ts % CONV_ROWS == 0 and CONV_HALO % SUBLANES == 0 and CONV_HALO >= DW_CONV_LEN
    blk = lambda c: pl.BlockSpec((None, ts, ch), lambda bi, si: (bi, si, c))
    return pl.pallas_call(
        functools.partial(_mixer_kernel, ts=ts, ch=ch),
        grid=(b, s // ts),
        in_specs=[blk(0), blk(1), blk(2), blk(3),
                  pl.BlockSpec((None,) + dww.shape[1:], lambda bi, si: (layer, 0, 0)),
                  pl.BlockSpec((None,) + pv.shape[1:], lambda bi, si: (layer, 0, 0)),
                  pl.BlockSpec((None, LRU_BLOCKS, bd, 2 * bd), lambda bi, si: (layer, 0, 0, 0))],
        out_specs=pl.BlockSpec((None, ts, 2 * ch), lambda bi, si: (bi, si, 0)),
        out_shape=jax.ShapeDtypeStruct((b, s, 2 * ch), BF16),
        scratch_shapes=[pltpu.VMEM((SUBLANES, CONV_HALO + ts, ch), F32),
                        pltpu.VMEM((LRU_HALO + ts, ch), F32),
                        pltpu.VMEM((SUBLANES, ch), F32)],
        compiler_params=pltpu.CompilerParams(
            dimension_semantics=("parallel", "arbitrary"),
            vmem_limit_bytes=_vmem_limit(
                [((ts, ch), F32)] * 4 + [(dww.shape[1:], F32), (pv.shape[1:], F32),
                                         ((LRU_BLOCKS, bd, 2 * bd), BF16), ((ts, 2 * ch), BF16)],
                scratch=[((SUBLANES, CONV_HALO + ts, ch), F32), ((LRU_HALO + ts, ch), F32)],
                temps=[((ts, ch), F32)] * 16)),
        name="seq_mixers",
    )(rest, rest, rest, rest, dww, pv, wai)


def _out_proj_kernel(ya_ref, mr_ref, h_ref, w1_ref, w2_ref, ga_ref, gpost_ref, gffn_ref,
                     hout_ref, u_ref):
    na = _rms(ya_ref[...], ga_ref[...]).astype(BF16)
    o = (jnp.dot(na, w1_ref[...], preferred_element_type=F32)
         + jnp.dot(mr_ref[...], w2_ref[...], preferred_element_type=F32))
    hn = h_ref[...] + _rms(o, gpost_ref[...])
    hout_ref[...] = hn
    u_ref[...] = _rms(hn, gffn_ref[...]).astype(u_ref.dtype)


def _out_proj(ya, mr, h, w_out, g_attn, g_post, g_ffn, layer, *, tm=256):
    m, d = h.shape
    ka = ya.shape[1]
    kr = mr.shape[1]
    assert ka == kr and ka + kr == w_out.shape[1]
    gain = lambda n: pl.BlockSpec((None, 1, n), lambda i: (layer, 0, 0))
    return pl.pallas_call(
        _out_proj_kernel,
        grid=(m // tm,),
        in_specs=[pl.BlockSpec((tm, ka), lambda i: (i, 0)),
                  pl.BlockSpec((tm, kr), lambda i: (i, 0)),
                  pl.BlockSpec((tm, d), lambda i: (i, 0)),
                  pl.BlockSpec((None, ka, d), lambda i: (layer, 0, 0)),
                  pl.BlockSpec((None, kr, d), lambda i: (layer, 1, 0)),
                  gain(ka), gain(d), gain(d)],
        out_specs=[pl.BlockSpec((tm, d), lambda i: (i, 0)),
                   pl.BlockSpec((tm, d), lambda i: (i, 0))],
        out_shape=[jax.ShapeDtypeStruct((m, d), F32), jax.ShapeDtypeStruct((m, d), BF16)],
        compiler_params=pltpu.CompilerParams(
            dimension_semantics=("parallel",),
            vmem_limit_bytes=_vmem_limit(
                [((tm, ka), F32), ((tm, kr), BF16), ((tm, d), F32), ((ka, d), BF16),
                 ((kr, d), BF16), ((tm, d), F32), ((tm, d), BF16)],
                temps=[((tm, d), F32)] * 4)),
        name="out_proj",
    )(ya, mr, h, w_out, w_out, g_attn, g_post, g_ffn)


def _ffn_kernel(u_ref, wg_ref, wu_ref, wd_ref, h_ref, gpost_ref, gnext_ref, *rest, emit_next):
    if emit_next:
        hout_ref, unext_ref, acc_ref = rest
    else:
        hout_ref, acc_ref = rest
    f = pl.program_id(1)

    @pl.when(f == 0)
    def _():
        acc_ref[...] = jnp.zeros_like(acc_ref)

    u = u_ref[...]
    g = jnp.dot(u, wg_ref[...], preferred_element_type=F32)
    up = jnp.dot(u, wu_ref[...], preferred_element_type=F32)
    act = (g * jax.nn.sigmoid(g) * up).astype(BF16)
    acc_ref[...] += jnp.dot(act, wd_ref[...], preferred_element_type=F32)

    @pl.when(f == pl.num_programs(1) - 1)
    def _():
        hn = h_ref[...] + _rms(acc_ref[...], gpost_ref[...])
        hout_ref[...] = hn
        if emit_next:
            unext_ref[...] = _rms(hn, gnext_ref[...]).astype(unext_ref.dtype)


def _ffn(u, h, wg, wu, wd, g_post, g_next, layer, next_layer, *, emit_next, tm=512, tf=512):
    m, d = h.shape
    dff = wg.shape[2]
    out_specs = [pl.BlockSpec((tm, d), lambda i, f: (i, 0))]
    out_shape = [jax.ShapeDtypeStruct((m, d), F32)]
    blocks = [((tm, d), BF16), ((d, tf), BF16), ((d, tf), BF16), ((tf, d), BF16),
              ((tm, d), F32), ((tm, d), F32)]
    if emit_next:
        out_specs.append(pl.BlockSpec((tm, d), lambda i, f: (i, 0)))
        out_shape.append(jax.ShapeDtypeStruct((m, d), BF16))
        blocks.append(((tm, d), BF16))
    return pl.pallas_call(
        functools.partial(_ffn_kernel, emit_next=emit_next),
        grid=(m // tm, dff // tf),
        in_specs=[pl.BlockSpec((tm, d), lambda i, f: (i, 0)),
                  pl.BlockSpec((None, d, tf), lambda i, f: (layer, 0, f)),
                  pl.BlockSpec((None, d, tf), lambda i, f: (layer, 0, f)),
                  pl.BlockSpec((None, tf, d), lambda i, f: (layer, f, 0)),
                  pl.BlockSpec((tm, d), lambda i, f: (i, 0)),
                  pl.BlockSpec((None, 1, d), lambda i, f: (layer, 0, 0)),
                  pl.BlockSpec((None, 1, d), lambda i, f: (next_layer, 0, 0))],
        out_specs=out_specs,
        out_shape=out_shape,
        scratch_shapes=[pltpu.VMEM((tm, d), F32)],
        compiler_params=pltpu.CompilerParams(
            dimension_semantics=("parallel", "arbitrary"),
            vmem_limit_bytes=_vmem_limit(blocks, scratch=[((tm, d), F32)],
                                         temps=[((tm, tf), F32)] * 3 + [((tm, d), F32)] * 2)),
        name="swiglu_ffn",
    )(u, wg, wu, wd, h, g_post, g_next)


def kernel(x, w_in, w_out, g_pre_mix, g_post_mix, g_pre_ffn, g_post_ffn, g_attn_grp,
           g_conv_grp, g_lru_grp, dw_conv_w, dw_conv_b, conv_ln_g, conv_ln_b, lru_conv_w,
           lru_conv_b, lru_w_a, lru_b_a, lru_w_i, lru_b_i, lru_lambda, w_gate, w_up, w_down):
    bsz, s_len, d = x.shape
    depth = w_in.shape[0]
    attn_w = g_attn_grp.shape[1]
    ch = g_conv_grp.shape[1]
    heads = attn_w // HEAD_DIM
    m = bsz * s_len
    assert w_in.shape[2] == 3 * attn_w + 4 * ch and lru_w_a.shape[1] == LRU_BLOCKS

    w_in_b = w_in.astype(BF16)
    w_out_b = w_out.astype(BF16)
    w_gate_b = w_gate.astype(BF16)
    w_up_b = w_up.astype(BF16)
    w_down_b = w_down.astype(BF16)
    wai = jnp.concatenate([lru_w_a, lru_w_i], axis=-1).astype(BF16)

    row3 = lambda v: v[:, None, :]
    pv = jnp.concatenate(
        [row3(dw_conv_b), row3(conv_ln_g), row3(conv_ln_b), row3(g_conv_grp), row3(lru_conv_b),
         row3(lru_b_a), row3(lru_b_i), row3(lru_lambda), row3(g_lru_grp), lru_conv_w,
         jnp.zeros((depth, 3, ch), F32)], axis=1)
    dww = jnp.concatenate([dw_conv_w, jnp.zeros((depth, 1, ch), F32)], axis=1)
    g_pre_mix3, g_post_mix3 = row3(g_pre_mix), row3(g_post_mix)
    g_pre_ffn3, g_post_ffn3 = row3(g_pre_ffn), row3(g_post_ffn)
    g_attn3 = row3(g_attn_grp)

    jj = lax.broadcasted_iota(jnp.int32, (ATTN_BLK, ATTN_BLK), 0)
    ss = lax.broadcasted_iota(jnp.int32, (ATTN_BLK, ATTN_BLK), 1)
    tri = jnp.concatenate([(jj > ss).astype(BF16), jnp.ones((ATTN_BLK, ATTN_BLK), BF16)], axis=1)
    tri = jnp.concatenate([tri, tri], axis=0)

    tn = 1024
    h = x.reshape(m, d)
    u = _prenorm(h, g_pre_mix3, 0)
    for l in range(depth):
        qkv = _in_proj(u, w_in_b, l, col_block0=0, n_out=3 * attn_w, out_dtype=BF16, tn=tn)
        rest = _in_proj(u, w_in_b, l, col_block0=3 * attn_w // tn, n_out=4 * ch,
                        out_dtype=F32, tn=tn)
        ya = _attention(qkv.reshape(bsz, s_len, 3 * attn_w), tri, heads=heads)
        mr = _mixers(rest.reshape(bsz, s_len, 4 * ch), dww, pv, wai, l)
        h, u2 = _out_proj(ya.reshape(m, attn_w), mr.reshape(m, 2 * ch), h, w_out_b,
                          g_attn3, g_post_mix3, g_pre_ffn3, l)
        last = l == depth - 1
        outs = _ffn(u2, h, w_gate_b, w_up_b, w_down_b, g_post_ffn3, g_pre_mix3, l,
                    0 if last else l + 1, emit_next=not last)
        if last:
            (h,) = outs
        else:
            h, u = outs
    return h.reshape(bsz, s_len, d)
```

```python
import functools

import jax
import jax.numpy as jnp
from jax import lax
from jax.experimental import pallas as pl
from jax.experimental.pallas import tpu as pltpu

F32 = jnp.float32
BF16 = jnp.bfloat16

EPS = 1e-6
HEAD_DIM = 128
DW_CONV_LEN = 31
LRU_CONV_LEN = 4
LRU_BLOCKS = 4
LRU_C = 8.0

V7X_VMEM_BYTES = 64 * 1024 * 1024
VMEM_CAP_BYTES = V7X_VMEM_BYTES - 8 * 1024 * 1024
SUBLANES = 8
LANES = 128

LOG_WEIGHT_FLOOR = -104.0

ATTN_BLK = LANES
ATTN_FIRST_BLOCKS = 3

CONV_HALO = 32
LRU_HALO = 8
CONV_ROWS = 64


def _nbytes(shape, dtype):
    n = 1
    for s in shape:
        n *= s
    return n * jnp.dtype(dtype).itemsize


def _vmem_limit(blocks, scratch=(), temps=(), single=()):
    total = 2 * sum(_nbytes(s, d) for s, d in blocks)
    total += sum(_nbytes(s, d) for s, d in single)
    total += sum(_nbytes(s, d) for s, d in scratch)
    total += sum(_nbytes(s, d) for s, d in temps)
    assert total <= VMEM_CAP_BYTES, total
    return total


def _rms(x, g):
    ms = jnp.mean(x * x, axis=-1, keepdims=True)
    return x * lax.rsqrt(ms + EPS) * g


def _softplus(z):
    return jnp.maximum(z, 0.0) + jnp.log(1.0 + jnp.exp(-jnp.abs(z)))


def _prenorm_kernel(x_ref, g_ref, o_ref):
    o_ref[...] = _rms(x_ref[...], g_ref[...]).astype(o_ref.dtype)


def _prenorm(h, g, layer, *, tm=512):
    m, d = h.shape
    return pl.pallas_call(
        _prenorm_kernel,
        grid=(m // tm,),
        in_specs=[pl.BlockSpec((tm, d), lambda i: (i, 0)),
                  pl.BlockSpec((None, 1, d), lambda i: (layer, 0, 0))],
        out_specs=pl.BlockSpec((tm, d), lambda i: (i, 0)),
        out_shape=jax.ShapeDtypeStruct((m, d), BF16),
        compiler_params=pltpu.CompilerParams(
            dimension_semantics=("parallel",),
            vmem_limit_bytes=_vmem_limit([((tm, d), F32), ((tm, d), BF16)],
                                         temps=[((tm, d), F32)] * 2)),
        name="prenorm",
    )(h, g)


def _in_proj_kernel(a_ref, w_ref, o_ref, wb_ref):
    @pl.when(pl.program_id(1) == 0)
    def _():
        wb_ref[...] = w_ref[...].astype(BF16)

    o_ref[...] = jnp.dot(a_ref[...], wb_ref[...],
                         preferred_element_type=F32).astype(o_ref.dtype)


def _in_proj(u, w, layer, *, col_block0, n_out, out_dtype, tm=1024, tn=1024):
    m, k = u.shape
    return pl.pallas_call(
        _in_proj_kernel,
        grid=(n_out // tn, m // tm),
        in_specs=[pl.BlockSpec((tm, k), lambda j, i: (i, 0)),
                  pl.BlockSpec((None, k, tn), lambda j, i: (layer, 0, j + col_block0))],
        out_specs=pl.BlockSpec((tm, tn), lambda j, i: (i, j)),
        out_shape=jax.ShapeDtypeStruct((m, n_out), out_dtype),
        scratch_shapes=[pltpu.VMEM((k, tn), BF16)],
        compiler_params=pltpu.CompilerParams(
            dimension_semantics=("parallel", "arbitrary"),
            vmem_limit_bytes=_vmem_limit(
                [((tm, k), BF16), ((k, tn), F32), ((tm, tn), out_dtype)],
                scratch=[((k, tn), BF16)], temps=[((tm, tn), F32)])),
        name="in_proj",
    )(u, w)


def _attn_kernel(q_ref, k_ref, v_ref, tri_ref, o_ref, carry_ref, *, units):
    tb = ATTN_BLK
    qi = pl.program_id(2)
    q0 = qi * (units * tb)
    tri = tri_ref[...]
    scale = HEAD_DIM ** -0.5
    tt = lax.broadcasted_iota(jnp.int32, (tb, tb), 0)
    ss = lax.broadcasted_iota(jnp.int32, (tb, tb), 1)
    below_diag = ss < tt

    def last_block(x, nblk, fn):
        head, tail = x[:, :(nblk - 1) * tb], fn(x[:, (nblk - 1) * tb:])
        return tail if nblk == 1 else jnp.concatenate([head, tail], axis=1)

    def sweep(starts, blocks, *, diag, carries=None, exists=None):
        rng = range(units)
        kbs = [k_ref[pl.ds(starts[r], blocks[r] * tb), :] for r in rng]
        vbs = [v_ref[pl.ds(starts[r], blocks[r] * tb), :] for r in rng]
        zs = [lax.dot_general(q_ref[r * tb:(r + 1) * tb, :], kbs[r], (((1,), (1,)), ((), ())),
                              preferred_element_type=F32) * scale for r in rng]
        drops, pieces = [], []
        for r in rng:
            nblk = blocks[r]
            drop = _softplus(zs[r])
            if diag:
                drop = last_block(drop, nblk, lambda x: jnp.where(below_diag, x, 0.0))
            if exists is not None:
                drop = jnp.where(exists[r], drop, 0.0)
            hi = drop.astype(BF16)
            lo = (drop - hi.astype(F32)).astype(BF16)
            drops.append(drop)
            pieces.append(jnp.concatenate(
                [jnp.concatenate([hi[:, j * tb:(j + 1) * tb], lo[:, j * tb:(j + 1) * tb]], axis=1)
                 for j in range(nblk)], axis=0))
        sums = [jnp.dot(pieces[r], tri, preferred_element_type=F32) for r in rng]
        ws, new_carries = [], []
        for r in rng:
            nblk = blocks[r]
            running = None if carries is None else carries[r]
            later = [None] * nblk
            for j in reversed(range(nblk)):
                local = sums[r][j * tb:(j + 1) * tb, :tb]
                total = sums[r][j * tb:(j + 1) * tb, tb:]
                later[j] = local if running is None else local + running
                running = total if running is None else running + total
            log_w = zs[r] - drops[r] - (later[0] if nblk == 1
                                        else jnp.concatenate(later, axis=1))
            if diag:
                log_w = last_block(log_w, nblk, lambda x: jnp.where(below_diag, x, -jnp.inf))
            w = jnp.exp(log_w)
            if exists is not None:
                w = jnp.where(exists[r], w, 0.0)
            ws.append(w.astype(BF16))
            new_carries.append(running)
        outs = [jnp.dot(ws[r], vbs[r], preferred_element_type=F32) for r in rng]
        for r in rng:
            rows = slice(r * tb, (r + 1) * tb)
            o_ref[rows, :] = outs[r] if carries is None else o_ref[rows, :] + outs[r]
            carry_ref[r] = new_carries[r]

    @pl.when(qi == 0)
    def _():
        blocks = [min(r + 1, ATTN_FIRST_BLOCKS) for r in range(units)]
        sweep([(r + 1 - n) * tb for r, n in enumerate(blocks)], blocks, diag=True)

    @pl.when(qi > 0)
    def _():
        sweep([pl.multiple_of(q0 + (r + 1 - ATTN_FIRST_BLOCKS) * tb, tb) for r in range(units)],
              [ATTN_FIRST_BLOCKS] * units, diag=True)

    def carry_min():
        m = carry_ref[0]
        for r in range(1, units):
            m = jnp.minimum(m, carry_ref[r])
        return jnp.min(m)

    def next_start(r, it):
        return q0 + (r - ATTN_FIRST_BLOCKS - it) * tb

    def cond(c):
        it, cmin = c
        return jnp.logical_and(next_start(units - 1, it) >= 0, cmin < -LOG_WEIGHT_FLOOR)

    def body(c):
        it, _ = c
        starts = [next_start(r, it) for r in range(units)]
        sweep([pl.multiple_of(jnp.maximum(k_lo, 0), tb) for k_lo in starts], [1] * units,
              diag=False, carries=[carry_ref[r] for r in range(units)],
              exists=[k_lo >= 0 for k_lo in starts])
        return it + 1, carry_min()

    lax.while_loop(cond, body, (jnp.int32(0), carry_min()))


def _attention(qkv, tri, *, heads, units=8):
    b, s, _ = qkv.shape
    dh = HEAD_DIM
    tq = units * ATTN_BLK
    return pl.pallas_call(
        functools.partial(_attn_kernel, units=units),
        grid=(b, heads, s // tq),
        in_specs=[pl.BlockSpec((None, tq, dh), lambda bi, h, i: (bi, i, h)),
                  pl.BlockSpec((None, s, dh), lambda bi, h, i: (bi, 0, heads + h)),
                  pl.BlockSpec((None, s, dh), lambda bi, h, i: (bi, 0, 2 * heads + h)),
                  pl.BlockSpec(tri.shape, lambda bi, h, i: (0, 0))],
        out_specs=pl.BlockSpec((None, tq, dh), lambda bi, h, i: (bi, i, h)),
        out_shape=jax.ShapeDtypeStruct((b, s, heads * dh), F32),
        scratch_shapes=[pltpu.VMEM((units, ATTN_BLK, ATTN_BLK), F32)],
        compiler_params=pltpu.CompilerParams(
            dimension_semantics=("parallel", "parallel", "parallel"),
            vmem_limit_bytes=_vmem_limit(
                [((tq, dh), BF16), ((s, dh), BF16), ((s, dh), BF16),
                 (tri.shape, BF16), ((tq, dh), F32)],
                scratch=[((units, ATTN_BLK, ATTN_BLK), F32)],
                temps=[((ATTN_BLK, ATTN_FIRST_BLOCKS * ATTN_BLK), F32)] * 8 * units)),
        name="stickbreak_attn",
    )(qkv, qkv, qkv, tri)


def _mixer_kernel(cv_ref, cg_ref, rx_ref, ry_ref, dww_ref, pv_ref, wai_ref, o_ref,
                  ush, rxbuf, hcarry, *, ts, ch):
    s = pl.program_id(1)

    @pl.when(s == 0)
    def _():
        ush[0, 0:CONV_HALO, :] = jnp.zeros((CONV_HALO, ch), F32)
        rxbuf[0:LRU_HALO, :] = jnp.zeros((LRU_HALO, ch), F32)
        hcarry[...] = jnp.zeros_like(hcarry)

    ush[0, CONV_HALO:CONV_HALO + ts, :] = cv_ref[...] * jax.nn.sigmoid(cg_ref[...])
    u_all = ush[0]
    for p in range(1, SUBLANES):
        ush[p] = pltpu.roll(u_all, p, 0)
    base = CONV_HALO - (DW_CONV_LEN - 1)
    for rt in range(ts // CONV_ROWS):
        r0 = rt * CONV_ROWS
        acc = jnp.broadcast_to(pv_ref[0:1, :], (CONV_ROWS, ch))
        for k in range(DW_CONV_LEN):
            p = -(base + k) % SUBLANES
            lo = base + k + p + r0
            acc = acc + dww_ref[k:k + 1, :] * ush[p, lo:lo + CONV_ROWS, :]
        mu = jnp.mean(acc, axis=-1, keepdims=True)
        xc = acc - mu
        var = jnp.mean(xc * xc, axis=-1, keepdims=True)
        y = xc * lax.rsqrt(var + EPS) * pv_ref[1:2, :] + pv_ref[2:3, :]
        y = y * jax.nn.sigmoid(y)
        o_ref[r0:r0 + CONV_ROWS, 0:ch] = _rms(y, pv_ref[3:4, :]).astype(o_ref.dtype)
    ush[0, 0:CONV_HALO, :] = ush[0, ts:ts + CONV_HALO, :]

    rxbuf[LRU_HALO:LRU_HALO + ts, :] = rx_ref[...]
    base = LRU_HALO - (LRU_CONV_LEN - 1)
    xr = jnp.broadcast_to(pv_ref[4:5, :], (ts, ch))
    for k in range(LRU_CONV_LEN):
        xr = xr + pv_ref[9 + k:10 + k, :] * rxbuf[base + k:base + k + ts, :]
    rxbuf[0:LRU_HALO, :] = rxbuf[ts:ts + LRU_HALO, :]
    xb = xr.astype(BF16)
    bd = ch // LRU_BLOCKS
    gates = [jnp.dot(xb[:, n * bd:(n + 1) * bd], wai_ref[n], preferred_element_type=F32)
             for n in range(LRU_BLOCKS)]
    gate_a = jnp.concatenate([g[:, :bd] for g in gates], axis=1) + pv_ref[5:6, :]
    gate_i = jnp.concatenate([g[:, bd:] for g in gates], axis=1) + pv_ref[6:7, :]
    r = jax.nn.sigmoid(gate_a)
    i = jax.nn.sigmoid(gate_i)
    log_a = -LRU_C * r * _softplus(-pv_ref[7:8, :])
    a = jnp.exp(log_a)
    b = jnp.sqrt(-jnp.tanh(log_a) * (a * a + 1.0)) * (i * xr)
    row = lax.broadcasted_iota(jnp.int32, (ts, ch), 0)
    d = 1
    while d < ts:
        a_prev = jnp.where(row >= d, pltpu.roll(a, d, 0), 1.0)
        b_prev = jnp.where(row >= d, pltpu.roll(b, d, 0), 0.0)
        b = a * b_prev + b
        a = a * a_prev
        d *= 2
    h = b + a * hcarry[0:1, :]
    hcarry[...] = jnp.broadcast_to(h[ts - 1:ts, :], hcarry.shape)
    y = h * jax.nn.gelu(ry_ref[...])
    o_ref[:, ch:2 * ch] = _rms(y, pv_ref[8:9, :]).astype(o_ref.dtype)


def _mixers(rest, dww, pv, wai, layer, *, ts=256):
    b, s, four_ch = rest.shape
    ch = four_ch // 4
    bd = ch // LRU_BLOCKS
    assert ts % CONV_ROWS == 0 and CONV_HALO % SUBLANES == 0 and CONV_HALO >= DW_CONV_LEN
    blk = lambda c: pl.BlockSpec((None, ts, ch), lambda bi, si: (bi, si, c))
    return pl.pallas_call(
        functools.partial(_mixer_kernel, ts=ts, ch=ch),
        grid=(b, s // ts),
        in_specs=[blk(0), blk(1), blk(2), blk(3),
                  pl.BlockSpec((None,) + dww.shape[1:], lambda bi, si: (layer, 0, 0)),
                  pl.BlockSpec((None,) + pv.shape[1:], lambda bi, si: (layer, 0, 0)),
                  pl.BlockSpec((None, LRU_BLOCKS, bd, 2 * bd), lambda bi, si: (layer, 0, 0, 0))],
        out_specs=pl.BlockSpec((None, ts, 2 * ch), lambda bi, si: (bi, si, 0)),
        out_shape=jax.ShapeDtypeStruct((b, s, 2 * ch), BF16),
        scratch_shapes=[pltpu.VMEM((SUBLANES, CONV_HALO + ts, ch), F32),
                        pltpu.VMEM((LRU_HALO + ts, ch), F32),
                        pltpu.VMEM((SUBLANES, ch), F32)],
        compiler_params=pltpu.CompilerParams(
            dimension_semantics=("parallel", "arbitrary"),
            vmem_limit_bytes=_vmem_limit(
                [((ts, ch), F32)] * 4 + [(dww.shape[1:], F32), (pv.shape[1:], F32),
                                         ((LRU_BLOCKS, bd, 2 * bd), BF16), ((ts, 2 * ch), BF16)],
                scratch=[((SUBLANES, CONV_HALO + ts, ch), F32), ((LRU_HALO + ts, ch), F32)],
                temps=[((ts, ch), F32)] * 16)),
        name="seq_mixers",
    )(rest, rest, rest, rest, dww, pv, wai)


def _out_proj_kernel(ya_ref, mr_ref, h_ref, w1_ref, w2_ref, ga_ref, gpost_ref, gffn_ref,
                     hout_ref, u_ref):
    na = _rms(ya_ref[...], ga_ref[...]).astype(BF16)
    o = (jnp.dot(na, w1_ref[...], preferred_element_type=F32)
         + jnp.dot(mr_ref[...], w2_ref[...], preferred_element_type=F32))
    hn = h_ref[...] + _rms(o, gpost_ref[...])
    hout_ref[...] = hn
    u_ref[...] = _rms(hn, gffn_ref[...]).astype(u_ref.dtype)


def _out_proj(ya, mr, h, w_out, g_attn, g_post, g_ffn, layer, *, tm=512):
    m, d = h.shape
    ka = ya.shape[1]
    kr = mr.shape[1]
    assert ka == kr and ka + kr == w_out.shape[1]
    gain = lambda n: pl.BlockSpec((None, 1, n), lambda i: (layer, 0, 0))
    weight = lambda rows, blk: pl.BlockSpec((None, rows, d), lambda i: (layer, blk, 0),
                                            pipeline_mode=pl.Buffered(1))
    return pl.pallas_call(
        _out_proj_kernel,
        grid=(m // tm,),
        in_specs=[pl.BlockSpec((tm, ka), lambda i: (i, 0)),
                  pl.BlockSpec((tm, kr), lambda i: (i, 0)),
                  pl.BlockSpec((tm, d), lambda i: (i, 0)),
                  weight(ka, 0), weight(kr, 1),
                  gain(ka), gain(d), gain(d)],
        out_specs=[pl.BlockSpec((tm, d), lambda i: (i, 0)),
                   pl.BlockSpec((tm, d), lambda i: (i, 0))],
        out_shape=[jax.ShapeDtypeStruct((m, d), F32), jax.ShapeDtypeStruct((m, d), BF16)],
        compiler_params=pltpu.CompilerParams(
            dimension_semantics=("parallel",),
            vmem_limit_bytes=_vmem_limit(
                [((tm, ka), F32), ((tm, kr), BF16), ((tm, d), F32),
                 ((tm, d), F32), ((tm, d), BF16)],
                single=[((ka, d), BF16), ((kr, d), BF16)],
                temps=[((tm, d), F32)] * 3)),
        name="out_proj",
    )(ya, mr, h, w_out, w_out, g_attn, g_post, g_ffn)


def _ffn_kernel(u_ref, wg_ref, wu_ref, wd_ref, h_ref, gpost_ref, *rest, emit_next):
    if emit_next:
        gnext_ref, hout_ref, unext_ref, acc_ref = rest
    else:
        hout_ref, acc_ref = rest
    f = pl.program_id(1)

    @pl.when(f == 0)
    def _():
        acc_ref[...] = jnp.zeros_like(acc_ref)

    u = u_ref[...]
    g = jnp.dot(u, wg_ref[...], preferred_element_type=F32)
    up = jnp.dot(u, wu_ref[...], preferred_element_type=F32)
    act = (g * jax.nn.sigmoid(g) * up).astype(BF16)
    acc_ref[...] += jnp.dot(act, wd_ref[...], preferred_element_type=F32)

    @pl.when(f == pl.num_programs(1) - 1)
    def _():
        hn = h_ref[...] + _rms(acc_ref[...], gpost_ref[...])
        hout_ref[...] = hn
        if emit_next:
            unext_ref[...] = _rms(hn, gnext_ref[...]).astype(unext_ref.dtype)


def _ffn(u, h, wg, wu, wd, g_post, g_next, layer, *, tm=512, tf=512):
    m, d = h.shape
    dff = wg.shape[2]
    emit_next = g_next is not None
    operands = [u, wg, wu, wd, h, g_post]
    in_specs = [pl.BlockSpec((tm, d), lambda i, f: (i, 0)),
                pl.BlockSpec((None, d, tf), lambda i, f: (layer, 0, f)),
                pl.BlockSpec((None, d, tf), lambda i, f: (layer, 0, f)),
                pl.BlockSpec((None, tf, d), lambda i, f: (layer, f, 0)),
                pl.BlockSpec((tm, d), lambda i, f: (i, 0)),
                pl.BlockSpec((None, 1, d), lambda i, f: (layer, 0, 0))]
    out_specs = [pl.BlockSpec((tm, d), lambda i, f: (i, 0))]
    out_shape = [jax.ShapeDtypeStruct((m, d), F32)]
    blocks = [((tm, d), BF16), ((d, tf), BF16), ((d, tf), BF16), ((tf, d), BF16),
              ((tm, d), F32), ((tm, d), F32)]
    if emit_next:
        operands.append(g_next)
        in_specs.append(pl.BlockSpec((None, 1, d), lambda i, f: (layer + 1, 0, 0)))
        out_specs.append(pl.BlockSpec((tm, d), lambda i, f: (i, 0)))
        out_shape.append(jax.ShapeDtypeStruct((m, d), BF16))
        blocks.append(((tm, d), BF16))
    return pl.pallas_call(
        functools.partial(_ffn_kernel, emit_next=emit_next),
        grid=(m // tm, dff // tf),
        in_specs=in_specs,
        out_specs=out_specs,
        out_shape=out_shape,
        scratch_shapes=[pltpu.VMEM((tm, d), F32)],
        compiler_params=pltpu.CompilerParams(
            dimension_semantics=("parallel", "arbitrary"),
            vmem_limit_bytes=_vmem_limit(blocks, scratch=[((tm, d), F32)],
                                         temps=[((tm, tf), F32)] * 3 + [((tm, d), F32)] * 2)),
        name="swiglu_ffn",
    )(*operands)


def kernel(x, w_in, w_out, g_pre_mix, g_post_mix, g_pre_ffn, g_post_ffn, g_attn_grp,
           g_conv_grp, g_lru_grp, dw_conv_w, dw_conv_b, conv_ln_g, conv_ln_b, lru_conv_w,
           lru_conv_b, lru_w_a, lru_b_a, lru_w_i, lru_b_i, lru_lambda, w_gate, w_up, w_down):
    bsz, s_len, d = x.shape
    depth = w_in.shape[0]
    attn_w = g_attn_grp.shape[1]
    ch = g_conv_grp.shape[1]
    heads = attn_w // HEAD_DIM
    m = bsz * s_len
    assert w_in.shape[2] == 3 * attn_w + 4 * ch and lru_w_a.shape[1] == LRU_BLOCKS

    w_out_b = w_out.astype(BF16)
    w_gate_b = w_gate.astype(BF16)
    w_up_b = w_up.astype(BF16)
    w_down_b = w_down.astype(BF16)
    wai = jnp.concatenate([lru_w_a, lru_w_i], axis=-1).astype(BF16)

    row3 = lambda v: v[:, None, :]
    pv = jnp.concatenate(
        [row3(dw_conv_b), row3(conv_ln_g), row3(conv_ln_b), row3(g_conv_grp), row3(lru_conv_b),
         row3(lru_b_a), row3(lru_b_i), row3(lru_lambda), row3(g_lru_grp), lru_conv_w,
         jnp.zeros((depth, 3, ch), F32)], axis=1)
    dww = jnp.concatenate([dw_conv_w, jnp.zeros((depth, 1, ch), F32)], axis=1)
    g_pre_mix3, g_post_mix3 = row3(g_pre_mix), row3(g_post_mix)
    g_pre_ffn3, g_post_ffn3 = row3(g_pre_ffn), row3(g_post_ffn)
    g_attn3 = row3(g_attn_grp)

    jj = lax.broadcasted_iota(jnp.int32, (ATTN_BLK, ATTN_BLK), 0)
    ss = lax.broadcasted_iota(jnp.int32, (ATTN_BLK, ATTN_BLK), 1)
    tri = jnp.concatenate([(jj > ss).astype(BF16), jnp.ones((ATTN_BLK, ATTN_BLK), BF16)], axis=1)
    tri = jnp.concatenate([tri, tri], axis=0)

    tn = 1024
    h = x.reshape(m, d)
    u = _prenorm(h, g_pre_mix3, 0)
    for l in range(depth):
        qkv = _in_proj(u, w_in, l, col_block0=0, n_out=3 * attn_w, out_dtype=BF16, tn=tn)
        rest = _in_proj(u, w_in, l, col_block0=3 * attn_w // tn, n_out=4 * ch,
                        out_dtype=F32, tn=tn)
        ya = _attention(qkv.reshape(bsz, s_len, 3 * attn_w), tri, heads=heads)
        mr = _mixers(rest.reshape(bsz, s_len, 4 * ch), dww, pv, wai, l)
        h, u2 = _out_proj(ya.reshape(m, attn_w), mr.reshape(m, 2 * ch), h, w_out_b,
                          g_attn3, g_post_mix3, g_pre_ffn3, l)
        if l == depth - 1:
            (h,) = _ffn(u2, h, w_gate_b, w_up_b, w_down_b, g_post_ffn3, None, l)
        else:
            h, u = _ffn(u2, h, w_gate_b, w_up_b, w_down_b, g_post_ffn3, g_pre_mix3, l)
    return h.reshape(bsz, s_len, d)
```

```python
import functools

import jax
import jax.numpy as jnp
from jax import lax
from jax.experimental import pallas as pl
from jax.experimental.pallas import tpu as pltpu

F32 = jnp.float32
BF16 = jnp.bfloat16

EPS = 1e-6
HEAD_DIM = 128
DW_CONV_LEN = 31
LRU_CONV_LEN = 4
LRU_BLOCKS = 4
LRU_C = 8.0

V7X_VMEM_BYTES = 64 * 1024 * 1024
VMEM_CAP_BYTES = V7X_VMEM_BYTES - 8 * 1024 * 1024
SUBLANES = 8
LANES = 128

LOG_WEIGHT_FLOOR = -104.0

ATTN_BLK = LANES
ATTN_FIRST_BLOCKS = 3

CONV_HALO = 32
LRU_HALO = 8
CONV_ROWS = 64


def _nbytes(shape, dtype):
    n = 1
    for s in shape:
        n *= s
    return n * jnp.dtype(dtype).itemsize


def _vmem_limit(blocks, scratch=(), temps=(), single=()):
    total = 2 * sum(_nbytes(s, d) for s, d in blocks)
    total += sum(_nbytes(s, d) for s, d in single)
    total += sum(_nbytes(s, d) for s, d in scratch)
    total += sum(_nbytes(s, d) for s, d in temps)
    assert total <= VMEM_CAP_BYTES, total
    return total


def _rms(x, g):
    ms = jnp.mean(x * x, axis=-1, keepdims=True)
    return x * lax.rsqrt(ms + EPS) * g


def _softplus(z):
    return jnp.maximum(z, 0.0) + jnp.log(1.0 + jnp.exp(-jnp.abs(z)))


def _prenorm_kernel(x_ref, g_ref, o_ref):
    o_ref[...] = _rms(x_ref[...], g_ref[...]).astype(o_ref.dtype)


def _prenorm(h, g, layer, *, tm=512):
    m, d = h.shape
    return pl.pallas_call(
        _prenorm_kernel,
        grid=(m // tm,),
        in_specs=[pl.BlockSpec((tm, d), lambda i: (i, 0)),
                  pl.BlockSpec((None, 1, d), lambda i: (layer, 0, 0))],
        out_specs=pl.BlockSpec((tm, d), lambda i: (i, 0)),
        out_shape=jax.ShapeDtypeStruct((m, d), BF16),
        compiler_params=pltpu.CompilerParams(
            dimension_semantics=("parallel",),
            vmem_limit_bytes=_vmem_limit([((tm, d), F32), ((tm, d), BF16)],
                                         temps=[((tm, d), F32)] * 2)),
        name="prenorm",
    )(h, g)


def _in_proj_kernel(a_ref, w_ref, o_ref, wb_ref):
    @pl.when(pl.program_id(1) == 0)
    def _():
        wb_ref[...] = w_ref[...].astype(BF16)

    o_ref[...] = jnp.dot(a_ref[...], wb_ref[...],
                         preferred_element_type=F32).astype(o_ref.dtype)


def _in_proj(u, w, layer, *, col_block0, n_out, out_dtype, tm=1024, tn=1024):
    m, k = u.shape
    return pl.pallas_call(
        _in_proj_kernel,
        grid=(n_out // tn, m // tm),
        in_specs=[pl.BlockSpec((tm, k), lambda j, i: (i, 0)),
                  pl.BlockSpec((None, k, tn), lambda j, i: (layer, 0, j + col_block0))],
        out_specs=pl.BlockSpec((tm, tn), lambda j, i: (i, j)),
        out_shape=jax.ShapeDtypeStruct((m, n_out), out_dtype),
        scratch_shapes=[pltpu.VMEM((k, tn), BF16)],
        compiler_params=pltpu.CompilerParams(
            dimension_semantics=("parallel", "arbitrary"),
            vmem_limit_bytes=_vmem_limit(
                [((tm, k), BF16), ((k, tn), F32), ((tm, tn), out_dtype)],
                scratch=[((k, tn), BF16)], temps=[((tm, tn), F32)])),
        name="in_proj",
    )(u, w)


def _attn_kernel(q_ref, k_ref, v_ref, tri_ref, o_ref, carry_ref, *, units):
    tb = ATTN_BLK
    qi = pl.program_id(2)
    q0 = qi * (units * tb)
    tri = tri_ref[...]
    scale = HEAD_DIM ** -0.5
    tt = lax.broadcasted_iota(jnp.int32, (tb, tb), 0)
    ss = lax.broadcasted_iota(jnp.int32, (tb, tb), 1)
    below_diag = ss < tt

    def last_block(x, nblk, fn):
        head, tail = x[:, :(nblk - 1) * tb], fn(x[:, (nblk - 1) * tb:])
        return tail if nblk == 1 else jnp.concatenate([head, tail], axis=1)

    def sweep(starts, blocks, *, diag, carries=None, exists=None):
        rng = range(units)
        kbs = [k_ref[pl.ds(starts[r], blocks[r] * tb), :] for r in rng]
        vbs = [v_ref[pl.ds(starts[r], blocks[r] * tb), :] for r in rng]
        zs = [lax.dot_general(q_ref[r * tb:(r + 1) * tb, :], kbs[r], (((1,), (1,)), ((), ())),
                              preferred_element_type=F32) * scale for r in rng]
        drops, pieces = [], []
        for r in rng:
            nblk = blocks[r]
            drop = _softplus(zs[r])
            if diag:
                drop = last_block(drop, nblk, lambda x: jnp.where(below_diag, x, 0.0))
            if exists is not None:
                drop = jnp.where(exists[r], drop, 0.0)
            hi = drop.astype(BF16)
            lo = (drop - hi.astype(F32)).astype(BF16)
            drops.append(drop)
            pieces.append(jnp.concatenate(
                [jnp.concatenate([hi[:, j * tb:(j + 1) * tb], lo[:, j * tb:(j + 1) * tb]], axis=1)
                 for j in range(nblk)], axis=0))
        sums = [jnp.dot(pieces[r], tri, preferred_element_type=F32) for r in rng]
        ws, new_carries = [], []
        for r in rng:
            nblk = blocks[r]
            running = None if carries is None else carries[r]
            later = [None] * nblk
            for j in reversed(range(nblk)):
                local = sums[r][j * tb:(j + 1) * tb, :tb]
                total = sums[r][j * tb:(j + 1) * tb, tb:]
                later[j] = local if running is None else local + running
                running = total if running is None else running + total
            log_w = zs[r] - drops[r] - (later[0] if nblk == 1
                                        else jnp.concatenate(later, axis=1))
            if diag:
                log_w = last_block(log_w, nblk, lambda x: jnp.where(below_diag, x, -jnp.inf))
            w = jnp.exp(log_w)
            if exists is not None:
                w = jnp.where(exists[r], w, 0.0)
            ws.append(w.astype(BF16))
            new_carries.append(running)
        outs = [jnp.dot(ws[r], vbs[r], preferred_element_type=F32) for r in rng]
        for r in rng:
            rows = slice(r * tb, (r + 1) * tb)
            o_ref[rows, :] = outs[r] if carries is None else o_ref[rows, :] + outs[r]
            carry_ref[r] = new_carries[r]

    @pl.when(qi == 0)
    def _():
        blocks = [min(r + 1, ATTN_FIRST_BLOCKS) for r in range(units)]
        sweep([(r + 1 - n) * tb for r, n in enumerate(blocks)], blocks, diag=True)

    @pl.when(qi > 0)
    def _():
        sweep([pl.multiple_of(q0 + (r + 1 - ATTN_FIRST_BLOCKS) * tb, tb) for r in range(units)],
              [ATTN_FIRST_BLOCKS] * units, diag=True)

    def next_start(r, it):
        return q0 + (r - ATTN_FIRST_BLOCKS - it) * tb

    def pending_min(it):
        m = None
        for r in range(units):
            c = jnp.where(next_start(r, it) >= 0, carry_ref[r], jnp.inf)
            m = c if m is None else jnp.minimum(m, c)
        return jnp.min(m)

    def cond(c):
        _, cmin = c
        return cmin < -LOG_WEIGHT_FLOOR

    def body(c):
        it, _ = c
        starts = [next_start(r, it) for r in range(units)]
        sweep([pl.multiple_of(jnp.maximum(k_lo, 0), tb) for k_lo in starts], [1] * units,
              diag=False, carries=[carry_ref[r] for r in range(units)],
              exists=[k_lo >= 0 for k_lo in starts])
        return it + 1, pending_min(it + 1)

    lax.while_loop(cond, body, (jnp.int32(0), pending_min(0)))


def _attention(qkv, tri, *, heads, units=16):
    b, s, _ = qkv.shape
    dh = HEAD_DIM
    tq = units * ATTN_BLK
    return pl.pallas_call(
        functools.partial(_attn_kernel, units=units),
        grid=(b, heads, s // tq),
        in_specs=[pl.BlockSpec((None, tq, dh), lambda bi, h, i: (bi, i, h)),
                  pl.BlockSpec((None, s, dh), lambda bi, h, i: (bi, 0, heads + h)),
                  pl.BlockSpec((None, s, dh), lambda bi, h, i: (bi, 0, 2 * heads + h)),
                  pl.BlockSpec(tri.shape, lambda bi, h, i: (0, 0))],
        out_specs=pl.BlockSpec((None, tq, dh), lambda bi, h, i: (bi, i, h)),
        out_shape=jax.ShapeDtypeStruct((b, s, heads * dh), F32),
        scratch_shapes=[pltpu.VMEM((units, ATTN_BLK, ATTN_BLK), F32)],
        compiler_params=pltpu.CompilerParams(
            dimension_semantics=("parallel", "parallel", "parallel"),
            vmem_limit_bytes=_vmem_limit(
                [((tq, dh), BF16), ((s, dh), BF16), ((s, dh), BF16),
                 (tri.shape, BF16), ((tq, dh), F32)],
                scratch=[((units, ATTN_BLK, ATTN_BLK), F32)],
                temps=[((ATTN_BLK, ATTN_FIRST_BLOCKS * ATTN_BLK), F32)] * 8 * units)),
        name="stickbreak_attn",
    )(qkv, qkv, qkv, tri)


def _mixer_kernel(cv_ref, cg_ref, rx_ref, ry_ref, dww_ref, pv_ref, wai_ref, o_ref,
                  ush, rxbuf, hcarry, wtile, *, ts, ch):
    s = pl.program_id(1)

    @pl.when(s == 0)
    def _():
        ush[0, 0:CONV_HALO, :] = jnp.zeros((CONV_HALO, ch), F32)
        rxbuf[0:LRU_HALO, :] = jnp.zeros((LRU_HALO, ch), F32)
        hcarry[...] = jnp.zeros_like(hcarry)
        for k in range(DW_CONV_LEN):
            wtile[k] = jnp.broadcast_to(dww_ref[k:k + 1, :], (SUBLANES, ch))

    ush[0, CONV_HALO:CONV_HALO + ts, :] = cv_ref[...] * jax.nn.sigmoid(cg_ref[...])
    u_all = ush[0]
    for p in range(1, SUBLANES):
        ush[p] = pltpu.roll(u_all, p, 0)
    base = CONV_HALO - (DW_CONV_LEN - 1)
    for rt in range(ts // CONV_ROWS):
        r0 = rt * CONV_ROWS
        acc = jnp.broadcast_to(pv_ref[0:1, :], (CONV_ROWS, ch))
        for k in range(DW_CONV_LEN):
            p = -(base + k) % SUBLANES
            lo = base + k + p + r0
            wk = jnp.concatenate([wtile[k]] * (CONV_ROWS // SUBLANES), axis=0)
            acc = acc + wk * ush[p, lo:lo + CONV_ROWS, :]
        mu = jnp.mean(acc, axis=-1, keepdims=True)
        xc = acc - mu
        var = jnp.mean(xc * xc, axis=-1, keepdims=True)
        y = xc * lax.rsqrt(var + EPS) * pv_ref[1:2, :] + pv_ref[2:3, :]
        y = y * jax.nn.sigmoid(y)
        o_ref[r0:r0 + CONV_ROWS, 0:ch] = _rms(y, pv_ref[3:4, :]).astype(o_ref.dtype)
    ush[0, 0:CONV_HALO, :] = ush[0, ts:ts + CONV_HALO, :]

    rxbuf[LRU_HALO:LRU_HALO + ts, :] = rx_ref[...]
    base = LRU_HALO - (LRU_CONV_LEN - 1)
    xr = jnp.broadcast_to(pv_ref[4:5, :], (ts, ch))
    for k in range(LRU_CONV_LEN):
        xr = xr + pv_ref[9 + k:10 + k, :] * rxbuf[base + k:base + k + ts, :]
    rxbuf[0:LRU_HALO, :] = rxbuf[ts:ts + LRU_HALO, :]
    xb = xr.astype(BF16)
    bd = ch // LRU_BLOCKS
    gates = [jnp.dot(xb[:, n * bd:(n + 1) * bd], wai_ref[n], preferred_element_type=F32)
             for n in range(LRU_BLOCKS)]
    gate_a = jnp.concatenate([g[:, :bd] for g in gates], axis=1) + pv_ref[5:6, :]
    gate_i = jnp.concatenate([g[:, bd:] for g in gates], axis=1) + pv_ref[6:7, :]
    r = jax.nn.sigmoid(gate_a)
    i = jax.nn.sigmoid(gate_i)
    log_a = -LRU_C * r * _softplus(-pv_ref[7:8, :])
    a = jnp.exp(log_a)
    b = jnp.sqrt(-jnp.tanh(log_a) * (a * a + 1.0)) * (i * xr)
    groups = ts // SUBLANES
    a = a.reshape(groups, SUBLANES, ch)
    b = b.reshape(groups, SUBLANES, ch)
    sub = lax.broadcasted_iota(jnp.int32, (groups, SUBLANES, ch), 1)
    d = 1
    while d < SUBLANES:
        a_prev = jnp.where(sub >= d, pltpu.roll(a, d, 1), 1.0)
        b_prev = jnp.where(sub >= d, pltpu.roll(b, d, 1), 0.0)
        b = a * b_prev + b
        a = a * a_prev
        d *= 2
    h_prev = hcarry[...]
    hs = []
    for gi in range(groups):
        hg = b[gi] + a[gi] * h_prev
        hs.append(hg)
        h_prev = jnp.broadcast_to(hg[SUBLANES - 1:SUBLANES, :], (SUBLANES, ch))
    hcarry[...] = h_prev
    h = jnp.concatenate(hs, axis=0)
    y = h * jax.nn.gelu(ry_ref[...])
    o_ref[:, ch:2 * ch] = _rms(y, pv_ref[8:9, :]).astype(o_ref.dtype)


def _mixers(rest, dww, pv, wai, layer, *, ts=256):
    b, s, four_ch = rest.shape
    ch = four_ch // 4
    bd = ch // LRU_BLOCKS
    assert ts % CONV_ROWS == 0 and CONV_HALO % SUBLANES == 0 and CONV_HALO >= DW_CONV_LEN
    blk = lambda c: pl.BlockSpec((None, ts, ch), lambda bi, si: (bi, si, c))
    return pl.pallas_call(
        functools.partial(_mixer_kernel, ts=ts, ch=ch),
        grid=(b, s // ts),
        in_specs=[blk(0), blk(1), blk(2), blk(3),
                  pl.BlockSpec((None,) + dww.shape[1:], lambda bi, si: (layer, 0, 0)),
                  pl.BlockSpec((None,) + pv.shape[1:], lambda bi, si: (layer, 0, 0)),
                  pl.BlockSpec((None, LRU_BLOCKS, bd, 2 * bd), lambda bi, si: (layer, 0, 0, 0))],
        out_specs=pl.BlockSpec((None, ts, 2 * ch), lambda bi, si: (bi, si, 0)),
        out_shape=jax.ShapeDtypeStruct((b, s, 2 * ch), BF16),
        scratch_shapes=[pltpu.VMEM((SUBLANES, CONV_HALO + ts, ch), F32),
                        pltpu.VMEM((LRU_HALO + ts, ch), F32),
                        pltpu.VMEM((SUBLANES, ch), F32),
                        pltpu.VMEM((DW_CONV_LEN, SUBLANES, ch), F32)],
        compiler_params=pltpu.CompilerParams(
            dimension_semantics=("parallel", "arbitrary"),
            vmem_limit_bytes=_vmem_limit(
                [((ts, ch), F32)] * 4 + [(dww.shape[1:], F32), (pv.shape[1:], F32),
                                         ((LRU_BLOCKS, bd, 2 * bd), BF16), ((ts, 2 * ch), BF16)],
                scratch=[((SUBLANES, CONV_HALO + ts, ch), F32), ((LRU_HALO + ts, ch), F32),
                         ((DW_CONV_LEN, SUBLANES, ch), F32)],
                temps=[((ts, ch), F32)] * 16)),
        name="seq_mixers",
    )(rest, rest, rest, rest, dww, pv, wai)


def _out_proj_kernel(ya_ref, mr_ref, h_ref, w1_ref, w2_ref, ga_ref, gpost_ref, gffn_ref,
                     hout_ref, u_ref):
    na = _rms(ya_ref[...], ga_ref[...]).astype(BF16)
    o = (jnp.dot(na, w1_ref[...], preferred_element_type=F32)
         + jnp.dot(mr_ref[...], w2_ref[...], preferred_element_type=F32))
    hn = h_ref[...] + _rms(o, gpost_ref[...])
    hout_ref[...] = hn
    u_ref[...] = _rms(hn, gffn_ref[...]).astype(u_ref.dtype)


def _out_proj(ya, mr, h, w_out, g_attn, g_post, g_ffn, layer, *, tm=512):
    m, d = h.shape
    ka = ya.shape[1]
    kr = mr.shape[1]
    assert ka == kr and ka + kr == w_out.shape[1]
    gain = lambda n: pl.BlockSpec((None, 1, n), lambda i: (layer, 0, 0))
    weight = lambda rows, blk: pl.BlockSpec((None, rows, d), lambda i: (layer, blk, 0),
                                            pipeline_mode=pl.Buffered(1))
    return pl.pallas_call(
        _out_proj_kernel,
        grid=(m // tm,),
        in_specs=[pl.BlockSpec((tm, ka), lambda i: (i, 0)),
                  pl.BlockSpec((tm, kr), lambda i: (i, 0)),
                  pl.BlockSpec((tm, d), lambda i: (i, 0)),
                  weight(ka, 0), weight(kr, 1),
                  gain(ka), gain(d), gain(d)],
        out_specs=[pl.BlockSpec((tm, d), lambda i: (i, 0)),
                   pl.BlockSpec((tm, d), lambda i: (i, 0))],
        out_shape=[jax.ShapeDtypeStruct((m, d), F32), jax.ShapeDtypeStruct((m, d), BF16)],
        compiler_params=pltpu.CompilerParams(
            dimension_semantics=("parallel",),
            vmem_limit_bytes=_vmem_limit(
                [((tm, ka), F32), ((tm, kr), BF16), ((tm, d), F32),
                 ((tm, d), F32), ((tm, d), BF16)],
                single=[((ka, d), BF16), ((kr, d), BF16)],
                temps=[((tm, d), F32)] * 3)),
        name="out_proj",
    )(ya, mr, h, w_out, w_out, g_attn, g_post, g_ffn)


def _ffn_kernel(u_ref, wg_ref, wu_ref, wd_ref, h_ref, gpost_ref, *rest, emit_next):
    if emit_next:
        gnext_ref, hout_ref, unext_ref, acc_ref = rest
    else:
        hout_ref, acc_ref = rest
    f = pl.program_id(1)

    @pl.when(f == 0)
    def _():
        acc_ref[...] = jnp.zeros_like(acc_ref)

    u = u_ref[...]
    g = jnp.dot(u, wg_ref[...], preferred_element_type=F32)
    up = jnp.dot(u, wu_ref[...], preferred_element_type=F32)
    act = (g * jax.nn.sigmoid(g) * up).astype(BF16)
    acc_ref[...] += jnp.dot(act, wd_ref[...], preferred_element_type=F32)

    @pl.when(f == pl.num_programs(1) - 1)
    def _():
        hn = h_ref[...] + _rms(acc_ref[...], gpost_ref[...])
        hout_ref[...] = hn
        if emit_next:
            unext_ref[...] = _rms(hn, gnext_ref[...]).astype(unext_ref.dtype)


def _ffn(u, h, wg, wu, wd, g_post, g_next, layer, *, tm=512, tf=512):
    m, d = h.shape
    dff = wg.shape[2]
    emit_next = g_next is not None
    operands = [u, wg, wu, wd, h, g_post]
    in_specs = [pl.BlockSpec((tm, d), lambda i, f: (i, 0)),
                pl.BlockSpec((None, d, tf), lambda i, f: (layer, 0, f)),
                pl.BlockSpec((None, d, tf), lambda i, f: (layer, 0, f)),
                pl.BlockSpec((None, tf, d), lambda i, f: (layer, f, 0)),
                pl.BlockSpec((tm, d), lambda i, f: (i, 0)),
                pl.BlockSpec((None, 1, d), lambda i, f: (layer, 0, 0))]
    out_specs = [pl.BlockSpec((tm, d), lambda i, f: (i, 0))]
    out_shape = [jax.ShapeDtypeStruct((m, d), F32)]
    blocks = [((tm, d), BF16), ((d, tf), BF16), ((d, tf), BF16), ((tf, d), BF16),
              ((tm, d), F32), ((tm, d), F32)]
    if emit_next:
        operands.append(g_next)
        in_specs.append(pl.BlockSpec((None, 1, d), lambda i, f: (layer + 1, 0, 0)))
        out_specs.append(pl.BlockSpec((tm, d), lambda i, f: (i, 0)))
        out_shape.append(jax.ShapeDtypeStruct((m, d), BF16))
        blocks.append(((tm, d), BF16))
    return pl.pallas_call(
        functools.partial(_ffn_kernel, emit_next=emit_next),
        grid=(m // tm, dff // tf),
        in_specs=in_specs,
        out_specs=out_specs,
        out_shape=out_shape,
        scratch_shapes=[pltpu.VMEM((tm, d), F32)],
        compiler_params=pltpu.CompilerParams(
            dimension_semantics=("parallel", "arbitrary"),
            vmem_limit_bytes=_vmem_limit(blocks, scratch=[((tm, d), F32)],
                                         temps=[((tm, tf), F32)] * 3 + [((tm, d), F32)] * 2)),
        name="swiglu_ffn",
    )(*operands)


def kernel(x, w_in, w_out, g_pre_mix, g_post_mix, g_pre_ffn, g_post_ffn, g_attn_grp,
           g_conv_grp, g_lru_grp, dw_conv_w, dw_conv_b, conv_ln_g, conv_ln_b, lru_conv_w,
           lru_conv_b, lru_w_a, lru_b_a, lru_w_i, lru_b_i, lru_lambda, w_gate, w_up, w_down):
    bsz, s_len, d = x.shape
    depth = w_in.shape[0]
    attn_w = g_attn_grp.shape[1]
    ch = g_conv_grp.shape[1]
    heads = attn_w // HEAD_DIM
    m = bsz * s_len
    assert w_in.shape[2] == 3 * attn_w + 4 * ch and lru_w_a.shape[1] == LRU_BLOCKS

    w_out_b = w_out.astype(BF16)
    w_gate_b = w_gate.astype(BF16)
    w_up_b = w_up.astype(BF16)
    w_down_b = w_down.astype(BF16)
    wai = jnp.concatenate([lru_w_a, lru_w_i], axis=-1).astype(BF16)

    row3 = lambda v: v[:, None, :]
    pv = jnp.concatenate(
        [row3(dw_conv_b), row3(conv_ln_g), row3(conv_ln_b), row3(g_conv_grp), row3(lru_conv_b),
         row3(lru_b_a), row3(lru_b_i), row3(lru_lambda), row3(g_lru_grp), lru_conv_w,
         jnp.zeros((depth, 3, ch), F32)], axis=1)
    dww = jnp.concatenate([dw_conv_w, jnp.zeros((depth, 1, ch), F32)], axis=1)
    g_pre_mix3, g_post_mix3 = row3(g_pre_mix), row3(g_post_mix)
    g_pre_ffn3, g_post_ffn3 = row3(g_pre_ffn), row3(g_post_ffn)
    g_attn3 = row3(g_attn_grp)

    jj = lax.broadcasted_iota(jnp.int32, (ATTN_BLK, ATTN_BLK), 0)
    ss = lax.broadcasted_iota(jnp.int32, (ATTN_BLK, ATTN_BLK), 1)
    tri = jnp.concatenate([(jj > ss).astype(BF16), jnp.ones((ATTN_BLK, ATTN_BLK), BF16)], axis=1)
    tri = jnp.concatenate([tri, tri], axis=0)

    tn = 1024
    h = x.reshape(m, d)
    u = _prenorm(h, g_pre_mix3, 0)
    for l in range(depth):
        qkv = _in_proj(u, w_in, l, col_block0=0, n_out=3 * attn_w, out_dtype=BF16, tn=tn)
        rest = _in_proj(u, w_in, l, col_block0=3 * attn_w // tn, n_out=4 * ch,
                        out_dtype=F32, tn=tn)
        ya = _attention(qkv.reshape(bsz, s_len, 3 * attn_w), tri, heads=heads)
        mr = _mixers(rest.reshape(bsz, s_len, 4 * ch), dww, pv, wai, l)
        h, u2 = _out_proj(ya.reshape(m, attn_w), mr.reshape(m, 2 * ch), h, w_out_b,
                          g_attn3, g_post_mix3, g_pre_ffn3, l)
        if l == depth - 1:
            (h,) = _ffn(u2, h, w_gate_b, w_up_b, w_down_b, g_post_ffn3, None, l)
        else:
            h, u = _ffn(u2, h, w_gate_b, w_up_b, w_down_b, g_post_ffn3, g_pre_mix3, l)
    return h.reshape(bsz, s_len, d)
```

```python
import functools
from typing import Callable, NamedTuple

import jax
import jax.numpy as jnp
from jax import lax
from jax.experimental import pallas as pl
from jax.experimental.pallas import tpu as pltpu

F32 = jnp.float32
BF16 = jnp.bfloat16

EPS = 1e-6
HEAD_DIM = 128
DW_CONV_LEN = 31
LRU_CONV_LEN = 4
LRU_BLOCKS = 4
LRU_C = 8.0

V7X_VMEM_BYTES = 64 * 1024 * 1024
VMEM_CAP_BYTES = V7X_VMEM_BYTES - 8 * 1024 * 1024
SUBLANES = 8
LANES = 128

LOG_WEIGHT_FLOOR = -104.0

ATTN_BLK = LANES
ATTN_FIRST_BLOCKS = 3

CONV_HALO = 32
LRU_HALO = 8
CONV_ROWS = 64


def _nbytes(shape, dtype):
    n = 1
    for s in shape:
        n *= s
    return n * jnp.dtype(dtype).itemsize


def _vmem_limit(blocks, scratch=(), temps=(), single=()):
    total = 2 * sum(_nbytes(s, d) for s, d in blocks)
    total += sum(_nbytes(s, d) for s, d in single)
    total += sum(_nbytes(s, d) for s, d in scratch)
    total += sum(_nbytes(s, d) for s, d in temps)
    assert total <= VMEM_CAP_BYTES, total
    return total


class _Cast(NamedTuple):
    src: jax.Array
    out_shape: tuple
    block: tuple
    in_index: Callable
    out_index: Callable


def _row_casts(srcs, layer, grid):
    steps = 1
    for g in grid:
        steps *= g

    def linear(*idx):
        step = idx[0]
        for g, i in zip(grid[1:], idx[1:]):
            step = step * g + i
        return step

    casts = []
    for src in srcs:
        _, rows, cols = src.shape
        block_rows = rows // steps
        assert block_rows * steps == rows and block_rows % (2 * SUBLANES) == 0
        casts.append(_Cast(src, (rows, cols), (block_rows, cols),
                           lambda *idx: (layer, linear(*idx), 0),
                           lambda *idx: (linear(*idx), 0)))
    return casts


def _with_casts(body, n_in, n_out, casts):
    n_cast = len(casts)

    def kernel_fn(*refs):
        ins, refs = refs[:n_in], refs[n_in:]
        cast_in, refs = refs[:n_cast], refs[n_cast:]
        outs, refs = refs[:n_out], refs[n_out:]
        cast_out, scratch = refs[:n_cast], refs[n_cast:]
        body(*ins, *outs, *scratch)
        for src, dst in zip(cast_in, cast_out):
            dst[...] = src[...].astype(dst.dtype)

    return (kernel_fn,
            [pl.BlockSpec((None,) + c.block, c.in_index) for c in casts],
            [pl.BlockSpec(c.block, c.out_index) for c in casts],
            [jax.ShapeDtypeStruct(c.out_shape, BF16) for c in casts],
            [c.src for c in casts],
            [(c.block, F32) for c in casts] + [(c.block, BF16) for c in casts])


def _rms(x, g):
    ms = jnp.mean(x * x, axis=-1, keepdims=True)
    return x * lax.rsqrt(ms + EPS) * g


def _softplus(z):
    return jnp.maximum(z, 0.0) + jnp.log(1.0 + jnp.exp(-jnp.abs(z)))


def _prenorm_kernel(x_ref, g_ref, o_ref):
    o_ref[...] = _rms(x_ref[...], g_ref[...]).astype(o_ref.dtype)


def _prenorm(h, g, layer, *, tm=512):
    m, d = h.shape
    return pl.pallas_call(
        _prenorm_kernel,
        grid=(m // tm,),
        in_specs=[pl.BlockSpec((tm, d), lambda i: (i, 0)),
                  pl.BlockSpec((None, 1, d), lambda i: (layer, 0, 0))],
        out_specs=pl.BlockSpec((tm, d), lambda i: (i, 0)),
        out_shape=jax.ShapeDtypeStruct((m, d), BF16),
        compiler_params=pltpu.CompilerParams(
            dimension_semantics=("parallel",),
            vmem_limit_bytes=_vmem_limit([((tm, d), F32), ((tm, d), BF16)],
                                         temps=[((tm, d), F32)] * 2)),
        name="prenorm",
    )(h, g)


def _in_proj_kernel(a_ref, w_ref, o_ref, wb_ref):
    @pl.when(pl.program_id(1) == 0)
    def _():
        wb_ref[...] = w_ref[...].astype(BF16)

    o_ref[...] = jnp.dot(a_ref[...], wb_ref[...],
                         preferred_element_type=F32).astype(o_ref.dtype)


def _in_proj(u, w, layer, *, col_block0, n_out, out_dtype, cast_srcs=(), tm=1024, tn=1024):
    m, k = u.shape
    grid = (n_out // tn, m // tm)
    body, c_in, c_out, c_shapes, c_ops, c_vmem = _with_casts(
        _in_proj_kernel, 2, 1, _row_casts(cast_srcs, layer, grid))
    return pl.pallas_call(
        body,
        grid=grid,
        in_specs=[pl.BlockSpec((tm, k), lambda j, i: (i, 0)),
                  pl.BlockSpec((None, k, tn), lambda j, i: (layer, 0, j + col_block0))] + c_in,
        out_specs=[pl.BlockSpec((tm, tn), lambda j, i: (i, j))] + c_out,
        out_shape=[jax.ShapeDtypeStruct((m, n_out), out_dtype)] + c_shapes,
        scratch_shapes=[pltpu.VMEM((k, tn), BF16)],
        compiler_params=pltpu.CompilerParams(
            dimension_semantics=("parallel", "arbitrary"),
            vmem_limit_bytes=_vmem_limit(
                [((tm, k), BF16), ((k, tn), F32), ((tm, tn), out_dtype)] + c_vmem,
                scratch=[((k, tn), BF16)], temps=[((tm, tn), F32)])),
        name="in_proj",
    )(u, w, *c_ops)


def _attn_kernel(q_ref, k_ref, v_ref, tri_ref, o_ref, carry_ref, *, units):
    tb = ATTN_BLK
    qi = pl.program_id(2)
    q0 = qi * (units * tb)
    tri = tri_ref[...]
    scale = HEAD_DIM ** -0.5
    tt = lax.broadcasted_iota(jnp.int32, (tb, tb), 0)
    ss = lax.broadcasted_iota(jnp.int32, (tb, tb), 1)
    below_diag = ss < tt

    def last_block(x, nblk, fn):
        head, tail = x[:, :(nblk - 1) * tb], fn(x[:, (nblk - 1) * tb:])
        return tail if nblk == 1 else jnp.concatenate([head, tail], axis=1)

    def sweep(starts, blocks, *, diag, carries=None, exists=None):
        rng = range(units)
        kbs = [k_ref[pl.ds(starts[r], blocks[r] * tb), :] for r in rng]
        vbs = [v_ref[pl.ds(starts[r], blocks[r] * tb), :] for r in rng]
        zs = [lax.dot_general(q_ref[r * tb:(r + 1) * tb, :], kbs[r], (((1,), (1,)), ((), ())),
                              preferred_element_type=F32) * scale for r in rng]
        drops, pieces = [], []
        for r in rng:
            nblk = blocks[r]
            drop = _softplus(zs[r])
            if diag:
                drop = last_block(drop, nblk, lambda x: jnp.where(below_diag, x, 0.0))
            if exists is not None:
                drop = jnp.where(exists[r], drop, 0.0)
            hi = drop.astype(BF16)
            lo = (drop - hi.astype(F32)).astype(BF16)
            drops.append(drop)
            pieces.append(jnp.concatenate(
                [jnp.concatenate([hi[:, j * tb:(j + 1) * tb], lo[:, j * tb:(j + 1) * tb]], axis=1)
                 for j in range(nblk)], axis=0))
        sums = [jnp.dot(pieces[r], tri, preferred_element_type=F32) for r in rng]
        ws, new_carries = [], []
        for r in rng:
            nblk = blocks[r]
            running = None if carries is None else carries[r]
            later = [None] * nblk
            for j in reversed(range(nblk)):
                local = sums[r][j * tb:(j + 1) * tb, :tb]
                total = sums[r][j * tb:(j + 1) * tb, tb:]
                later[j] = local if running is None else local + running
                running = total if running is None else running + total
            log_w = zs[r] - drops[r] - (later[0] if nblk == 1
                                        else jnp.concatenate(later, axis=1))
            if diag:
                log_w = last_block(log_w, nblk, lambda x: jnp.where(below_diag, x, -jnp.inf))
            w = jnp.exp(log_w)
            if exists is not None:
                w = jnp.where(exists[r], w, 0.0)
            ws.append(w.astype(BF16))
            new_carries.append(running)
        outs = [jnp.dot(ws[r], vbs[r], preferred_element_type=F32) for r in rng]
        for r in rng:
            rows = slice(r * tb, (r + 1) * tb)
            o_ref[rows, :] = outs[r] if carries is None else o_ref[rows, :] + outs[r]
            carry_ref[r] = new_carries[r]

    @pl.when(qi == 0)
    def _():
        blocks = [min(r + 1, ATTN_FIRST_BLOCKS) for r in range(units)]
        sweep([(r + 1 - n) * tb for r, n in enumerate(blocks)], blocks, diag=True)

    @pl.when(qi > 0)
    def _():
        sweep([pl.multiple_of(q0 + (r + 1 - ATTN_FIRST_BLOCKS) * tb, tb) for r in range(units)],
              [ATTN_FIRST_BLOCKS] * units, diag=True)

    def next_start(r, it):
        return q0 + (r - ATTN_FIRST_BLOCKS - it) * tb

    def pending_min(it):
        m = None
        for r in range(units):
            c = jnp.where(next_start(r, it) >= 0, carry_ref[r], jnp.inf)
            m = c if m is None else jnp.minimum(m, c)
        return jnp.min(m)

    def cond(c):
        _, cmin = c
        return cmin < -LOG_WEIGHT_FLOOR

    def body(c):
        it, _ = c
        starts = [next_start(r, it) for r in range(units)]
        sweep([pl.multiple_of(jnp.maximum(k_lo, 0), tb) for k_lo in starts], [1] * units,
              diag=False, carries=[carry_ref[r] for r in range(units)],
              exists=[k_lo >= 0 for k_lo in starts])
        return it + 1, pending_min(it + 1)

    lax.while_loop(cond, body, (jnp.int32(0), pending_min(0)))


def _attention(qkv, tri, *, heads, cast_srcs=(), cast_layer=0, units=16):
    b, s, _ = qkv.shape
    dh = HEAD_DIM
    tq = units * ATTN_BLK
    grid = (b, heads, s // tq)
    body, c_in, c_out, c_shapes, c_ops, c_vmem = _with_casts(
        functools.partial(_attn_kernel, units=units), 4, 1,
        _row_casts(cast_srcs, cast_layer, grid))
    return pl.pallas_call(
        body,
        grid=grid,
        in_specs=[pl.BlockSpec((None, tq, dh), lambda bi, h, i: (bi, i, h)),
                  pl.BlockSpec((None, s, dh), lambda bi, h, i: (bi, 0, heads + h)),
                  pl.BlockSpec((None, s, dh), lambda bi, h, i: (bi, 0, 2 * heads + h)),
                  pl.BlockSpec(tri.shape, lambda bi, h, i: (0, 0))] + c_in,
        out_specs=[pl.BlockSpec((None, tq, dh), lambda bi, h, i: (bi, i, h))] + c_out,
        out_shape=[jax.ShapeDtypeStruct((b, s, heads * dh), F32)] + c_shapes,
        scratch_shapes=[pltpu.VMEM((units, ATTN_BLK, ATTN_BLK), F32)],
        compiler_params=pltpu.CompilerParams(
            dimension_semantics=("parallel", "parallel", "parallel"),
            vmem_limit_bytes=_vmem_limit(
                [((tq, dh), BF16), ((s, dh), BF16), ((s, dh), BF16),
                 (tri.shape, BF16), ((tq, dh), F32)] + c_vmem,
                scratch=[((units, ATTN_BLK, ATTN_BLK), F32)],
                temps=[((ATTN_BLK, ATTN_FIRST_BLOCKS * ATTN_BLK), F32)] * 8 * units)),
        name="stickbreak_attn",
    )(qkv, qkv, qkv, tri, *c_ops)


def _mixer_kernel(cv_ref, cg_ref, rx_ref, ry_ref, dww_ref, pv_ref, wai_ref, o_ref,
                  ush, rxbuf, hcarry, wtile, *, ts, ch):
    s = pl.program_id(1)

    @pl.when(s == 0)
    def _():
        ush[0, 0:CONV_HALO, :] = jnp.zeros((CONV_HALO, ch), F32)
        rxbuf[0:LRU_HALO, :] = jnp.zeros((LRU_HALO, ch), F32)
        hcarry[...] = jnp.zeros_like(hcarry)
        for k in range(DW_CONV_LEN):
            wtile[k] = jnp.broadcast_to(dww_ref[k:k + 1, :], (SUBLANES, ch))

    ush[0, CONV_HALO:CONV_HALO + ts, :] = cv_ref[...] * jax.nn.sigmoid(cg_ref[...])
    u_all = ush[0]
    for p in range(1, SUBLANES):
        ush[p] = pltpu.roll(u_all, p, 0)
    base = CONV_HALO - (DW_CONV_LEN - 1)
    for rt in range(ts // CONV_ROWS):
        r0 = rt * CONV_ROWS
        acc = jnp.broadcast_to(pv_ref[0:1, :], (CONV_ROWS, ch))
        for k in range(DW_CONV_LEN):
            p = -(base + k) % SUBLANES
            lo = base + k + p + r0
            wk = jnp.concatenate([wtile[k]] * (CONV_ROWS // SUBLANES), axis=0)
            acc = acc + wk * ush[p, lo:lo + CONV_ROWS, :]
        mu = jnp.mean(acc, axis=-1, keepdims=True)
        xc = acc - mu
        var = jnp.mean(xc * xc, axis=-1, keepdims=True)
        y = xc * lax.rsqrt(var + EPS) * pv_ref[1:2, :] + pv_ref[2:3, :]
        y = y * jax.nn.sigmoid(y)
        o_ref[r0:r0 + CONV_ROWS, 0:ch] = _rms(y, pv_ref[3:4, :]).astype(o_ref.dtype)
    ush[0, 0:CONV_HALO, :] = ush[0, ts:ts + CONV_HALO, :]

    rxbuf[LRU_HALO:LRU_HALO + ts, :] = rx_ref[...]
    base = LRU_HALO - (LRU_CONV_LEN - 1)
    xr = jnp.broadcast_to(pv_ref[4:5, :], (ts, ch))
    for k in range(LRU_CONV_LEN):
        xr = xr + pv_ref[9 + k:10 + k, :] * rxbuf[base + k:base + k + ts, :]
    rxbuf[0:LRU_HALO, :] = rxbuf[ts:ts + LRU_HALO, :]
    xb = xr.astype(BF16)
    bd = ch // LRU_BLOCKS
    gates = [jnp.dot(xb[:, n * bd:(n + 1) * bd], wai_ref[n], preferred_element_type=F32)
             for n in range(LRU_BLOCKS)]
    gate_a = jnp.concatenate([g[:, :bd] for g in gates], axis=1) + pv_ref[5:6, :]
    gate_i = jnp.concatenate([g[:, bd:] for g in gates], axis=1) + pv_ref[6:7, :]
    r = jax.nn.sigmoid(gate_a)
    i = jax.nn.sigmoid(gate_i)
    log_a = -LRU_C * r * _softplus(-pv_ref[7:8, :])
    a = jnp.exp(log_a)
    b = jnp.sqrt(-jnp.tanh(log_a) * (a * a + 1.0)) * (i * xr)
    groups = ts // SUBLANES
    a = a.reshape(groups, SUBLANES, ch)
    b = b.reshape(groups, SUBLANES, ch)
    sub = lax.broadcasted_iota(jnp.int32, (groups, SUBLANES, ch), 1)
    d = 1
    while d < SUBLANES:
        a_prev = jnp.where(sub >= d, pltpu.roll(a, d, 1), 1.0)
        b_prev = jnp.where(sub >= d, pltpu.roll(b, d, 1), 0.0)
        b = a * b_prev + b
        a = a * a_prev
        d *= 2
    h_prev = hcarry[...]
    hs = []
    for gi in range(groups):
        hg = b[gi] + a[gi] * h_prev
        hs.append(hg)
        h_prev = jnp.broadcast_to(hg[SUBLANES - 1:SUBLANES, :], (SUBLANES, ch))
    hcarry[...] = h_prev
    h = jnp.concatenate(hs, axis=0)
    y = h * jax.nn.gelu(ry_ref[...])
    o_ref[:, ch:2 * ch] = _rms(y, pv_ref[8:9, :]).astype(o_ref.dtype)


def _mixers(rest, dww, pv, wai, layer, *, cast_srcs=(), ts=256):
    b, s, four_ch = rest.shape
    ch = four_ch // 4
    bd = ch // LRU_BLOCKS
    assert ts % CONV_ROWS == 0 and CONV_HALO % SUBLANES == 0 and CONV_HALO >= DW_CONV_LEN
    blk = lambda c: pl.BlockSpec((None, ts, ch), lambda bi, si: (bi, si, c))
    grid = (b, s // ts)
    body, c_in, c_out, c_shapes, c_ops, c_vmem = _with_casts(
        functools.partial(_mixer_kernel, ts=ts, ch=ch), 7, 1, _row_casts(cast_srcs, layer, grid))
    return pl.pallas_call(
        body,
        grid=grid,
        in_specs=[blk(0), blk(1), blk(2), blk(3),
                  pl.BlockSpec((None,) + dww.shape[1:], lambda bi, si: (layer, 0, 0)),
                  pl.BlockSpec((None,) + pv.shape[1:], lambda bi, si: (layer, 0, 0)),
                  pl.BlockSpec((None, LRU_BLOCKS, bd, 2 * bd),
                               lambda bi, si: (layer, 0, 0, 0))] + c_in,
        out_specs=[pl.BlockSpec((None, ts, 2 * ch), lambda bi, si: (bi, si, 0))] + c_out,
        out_shape=[jax.ShapeDtypeStruct((b, s, 2 * ch), BF16)] + c_shapes,
        scratch_shapes=[pltpu.VMEM((SUBLANES, CONV_HALO + ts, ch), F32),
                        pltpu.VMEM((LRU_HALO + ts, ch), F32),
                        pltpu.VMEM((SUBLANES, ch), F32),
                        pltpu.VMEM((DW_CONV_LEN, SUBLANES, ch), F32)],
        compiler_params=pltpu.CompilerParams(
            dimension_semantics=("parallel", "arbitrary"),
            vmem_limit_bytes=_vmem_limit(
                [((ts, ch), F32)] * 4 + [(dww.shape[1:], F32), (pv.shape[1:], F32),
                                         ((LRU_BLOCKS, bd, 2 * bd), BF16), ((ts, 2 * ch), BF16)]
                + c_vmem,
                scratch=[((SUBLANES, CONV_HALO + ts, ch), F32), ((LRU_HALO + ts, ch), F32),
                         ((DW_CONV_LEN, SUBLANES, ch), F32)],
                temps=[((ts, ch), F32)] * 16)),
        name="seq_mixers",
    )(rest, rest, rest, rest, dww, pv, wai, *c_ops)


def _out_proj_kernel(ya_ref, mr_ref, h_ref, w1_ref, w2_ref, ga_ref, gpost_ref, gffn_ref,
                     hout_ref, u_ref):
    na = _rms(ya_ref[...], ga_ref[...]).astype(BF16)
    o = (jnp.dot(na, w1_ref[...], preferred_element_type=F32)
         + jnp.dot(mr_ref[...], w2_ref[...], preferred_element_type=F32))
    hn = h_ref[...] + _rms(o, gpost_ref[...])
    hout_ref[...] = hn
    u_ref[...] = _rms(hn, gffn_ref[...]).astype(u_ref.dtype)


def _out_proj(ya, mr, h, w_out, g_attn, g_post, g_ffn, layer, *, tm=512):
    m, d = h.shape
    ka = ya.shape[1]
    kr = mr.shape[1]
    assert ka == kr and w_out.shape == (ka + kr, d)
    gain = lambda n: pl.BlockSpec((None, 1, n), lambda i: (layer, 0, 0))
    weight = lambda rows, blk: pl.BlockSpec((rows, d), lambda i: (blk, 0),
                                            pipeline_mode=pl.Buffered(1))
    return pl.pallas_call(
        _out_proj_kernel,
        grid=(m // tm,),
        in_specs=[pl.BlockSpec((tm, ka), lambda i: (i, 0)),
                  pl.BlockSpec((tm, kr), lambda i: (i, 0)),
                  pl.BlockSpec((tm, d), lambda i: (i, 0)),
                  weight(ka, 0), weight(kr, 1),
                  gain(ka), gain(d), gain(d)],
        out_specs=[pl.BlockSpec((tm, d), lambda i: (i, 0)),
                   pl.BlockSpec((tm, d), lambda i: (i, 0))],
        out_shape=[jax.ShapeDtypeStruct((m, d), F32), jax.ShapeDtypeStruct((m, d), BF16)],
        compiler_params=pltpu.CompilerParams(
            dimension_semantics=("parallel",),
            vmem_limit_bytes=_vmem_limit(
                [((tm, ka), F32), ((tm, kr), BF16), ((tm, d), F32),
                 ((tm, d), F32), ((tm, d), BF16)],
                single=[((ka, d), BF16), ((kr, d), BF16)],
                temps=[((tm, d), F32)] * 3)),
        name="out_proj",
    )(ya, mr, h, w_out, w_out, g_attn, g_post, g_ffn)


def _ffn_kernel(u_ref, wg_ref, wu_ref, wd_ref, h_ref, gpost_ref, *rest, emit_next):
    if emit_next:
        gnext_ref, hout_ref, unext_ref, acc_ref = rest
    else:
        hout_ref, acc_ref = rest
    f = pl.program_id(1)

    @pl.when(f == 0)
    def _():
        acc_ref[...] = jnp.zeros_like(acc_ref)

    u = u_ref[...]
    g = jnp.dot(u, wg_ref[...], preferred_element_type=F32)
    up = jnp.dot(u, wu_ref[...], preferred_element_type=F32)
    act = (g * jax.nn.sigmoid(g) * up).astype(BF16)
    acc_ref[...] += jnp.dot(act, wd_ref[...], preferred_element_type=F32)

    @pl.when(f == pl.num_programs(1) - 1)
    def _():
        hn = h_ref[...] + _rms(acc_ref[...], gpost_ref[...])
        hout_ref[...] = hn
        if emit_next:
            unext_ref[...] = _rms(hn, gnext_ref[...]).astype(unext_ref.dtype)


def _ffn(u, h, wg, wu, wd, g_post, g_next, layer, *, next_weights=(), tm=512, tf=512):
    m, d = h.shape
    dff = wg.shape[1]
    emit_next = g_next is not None
    grid = (m // tm, dff // tf)
    casts = []
    if next_weights:
        w_gate, w_up, w_down = next_weights
        col = d // grid[0]
        casts = [_Cast(w, (d, dff), (col, tf), lambda i, f: (layer + 1, i, f), lambda i, f: (i, f))
                 for w in (w_gate, w_up)]
        casts.append(_Cast(w_down, (dff, d), (tf, col), lambda i, f: (layer + 1, f, i),
                           lambda i, f: (f, i)))
    n_in = 7 if emit_next else 6
    body, c_in, c_out, c_shapes, c_ops, c_vmem = _with_casts(
        functools.partial(_ffn_kernel, emit_next=emit_next), n_in, 2 if emit_next else 1, casts)
    operands = [u, wg, wu, wd, h, g_post]
    in_specs = [pl.BlockSpec((tm, d), lambda i, f: (i, 0)),
                pl.BlockSpec((d, tf), lambda i, f: (0, f)),
                pl.BlockSpec((d, tf), lambda i, f: (0, f)),
                pl.BlockSpec((tf, d), lambda i, f: (f, 0)),
                pl.BlockSpec((tm, d), lambda i, f: (i, 0)),
                pl.BlockSpec((None, 1, d), lambda i, f: (layer, 0, 0))]
    out_specs = [pl.BlockSpec((tm, d), lambda i, f: (i, 0))]
    out_shape = [jax.ShapeDtypeStruct((m, d), F32)]
    blocks = [((tm, d), BF16), ((d, tf), BF16), ((d, tf), BF16), ((tf, d), BF16),
              ((tm, d), F32), ((tm, d), F32)]
    if emit_next:
        operands.append(g_next)
        in_specs.append(pl.BlockSpec((None, 1, d), lambda i, f: (layer + 1, 0, 0)))
        out_specs.append(pl.BlockSpec((tm, d), lambda i, f: (i, 0)))
        out_shape.append(jax.ShapeDtypeStruct((m, d), BF16))
        blocks.append(((tm, d), BF16))
    assert len(in_specs) == n_in
    return pl.pallas_call(
        body,
        grid=grid,
        in_specs=in_specs + c_in,
        out_specs=out_specs + c_out,
        out_shape=out_shape + c_shapes,
        scratch_shapes=[pltpu.VMEM((tm, d), F32)],
        compiler_params=pltpu.CompilerParams(
            dimension_semantics=("parallel", "arbitrary"),
            vmem_limit_bytes=_vmem_limit(blocks + c_vmem, scratch=[((tm, d), F32)],
                                         temps=[((tm, tf), F32)] * 3 + [((tm, d), F32)] * 2)),
        name="swiglu_ffn",
    )(*operands, *c_ops)


def kernel(x, w_in, w_out, g_pre_mix, g_post_mix, g_pre_ffn, g_post_ffn, g_attn_grp,
           g_conv_grp, g_lru_grp, dw_conv_w, dw_conv_b, conv_ln_g, conv_ln_b, lru_conv_w,
           lru_conv_b, lru_w_a, lru_b_a, lru_w_i, lru_b_i, lru_lambda, w_gate, w_up, w_down):
    bsz, s_len, d = x.shape
    depth = w_in.shape[0]
    attn_w = g_attn_grp.shape[1]
    ch = g_conv_grp.shape[1]
    heads = attn_w // HEAD_DIM
    m = bsz * s_len
    assert w_in.shape[2] == 3 * attn_w + 4 * ch and lru_w_a.shape[1] == LRU_BLOCKS

    wai = jnp.concatenate([lru_w_a, lru_w_i], axis=-1).astype(BF16)

    row3 = lambda v: v[:, None, :]
    pv = jnp.concatenate(
        [row3(dw_conv_b), row3(conv_ln_g), row3(conv_ln_b), row3(g_conv_grp), row3(lru_conv_b),
         row3(lru_b_a), row3(lru_b_i), row3(lru_lambda), row3(g_lru_grp), lru_conv_w,
         jnp.zeros((depth, 3, ch), F32)], axis=1)
    dww = jnp.concatenate([dw_conv_w, jnp.zeros((depth, 1, ch), F32)], axis=1)
    g_pre_mix3, g_post_mix3 = row3(g_pre_mix), row3(g_post_mix)
    g_pre_ffn3, g_post_ffn3 = row3(g_pre_ffn), row3(g_post_ffn)
    g_attn3 = row3(g_attn_grp)

    jj = lax.broadcasted_iota(jnp.int32, (ATTN_BLK, ATTN_BLK), 0)
    ss = lax.broadcasted_iota(jnp.int32, (ATTN_BLK, ATTN_BLK), 1)
    tri = jnp.concatenate([(jj > ss).astype(BF16), jnp.ones((ATTN_BLK, ATTN_BLK), BF16)], axis=1)
    tri = jnp.concatenate([tri, tri], axis=0)

    tn = 1024
    h = x.reshape(m, d)
    u = _prenorm(h, g_pre_mix3, 0)
    for l in range(depth):
        (qkv,) = _in_proj(u, w_in, l, col_block0=0, n_out=3 * attn_w, out_dtype=BF16, tn=tn)
        rest, w_out_b = _in_proj(u, w_in, l, col_block0=3 * attn_w // tn, n_out=4 * ch,
                                 out_dtype=F32, cast_srcs=(w_out,), tn=tn)
        qkv3, rest3 = qkv.reshape(bsz, s_len, 3 * attn_w), rest.reshape(bsz, s_len, 4 * ch)
        if l == 0:
            ya, w_gate_b, w_up_b = _attention(qkv3, tri, heads=heads, cast_srcs=(w_gate, w_up))
            mr, w_down_b = _mixers(rest3, dww, pv, wai, l, cast_srcs=(w_down,))
        else:
            (ya,) = _attention(qkv3, tri, heads=heads)
            (mr,) = _mixers(rest3, dww, pv, wai, l)
        h, u2 = _out_proj(ya.reshape(m, attn_w), mr.reshape(m, 2 * ch), h, w_out_b,
                          g_attn3, g_post_mix3, g_pre_ffn3, l)
        if l == depth - 1:
            (h,) = _ffn(u2, h, w_gate_b, w_up_b, w_down_b, g_post_ffn3, None, l)
        else:
            h, u, w_gate_b, w_up_b, w_down_b = _ffn(
                u2, h, w_gate_b, w_up_b, w_down_b, g_post_ffn3, g_pre_mix3, l,
                next_weights=(w_gate, w_up, w_down))
    return h.reshape(bsz, s_len, d)
```

```python
import functools
from typing import Callable, NamedTuple

import jax
import jax.numpy as jnp
from jax import lax
from jax.experimental import pallas as pl
from jax.experimental.pallas import tpu as pltpu

F32 = jnp.float32
BF16 = jnp.bfloat16

EPS = 1e-6
HEAD_DIM = 128
DW_CONV_LEN = 31
LRU_CONV_LEN = 4
LRU_BLOCKS = 4
LRU_C = 8.0

V7X_VMEM_BYTES = 64 * 1024 * 1024
VMEM_CAP_BYTES = V7X_VMEM_BYTES - 8 * 1024 * 1024
SUBLANES = 8
LANES = 128

LOG_WEIGHT_FLOOR = -104.0

ATTN_BLK = LANES
ATTN_FIRST_BLOCKS = 3

CONV_HALO = 32
LRU_HALO = 8
CONV_ROWS = 64
OUT_PROJ_GROUPS = 2


def _nbytes(shape, dtype):
    n = 1
    for s in shape:
        n *= s
    return n * jnp.dtype(dtype).itemsize


def _vmem_limit(blocks, scratch=(), temps=(), single=()):
    total = 2 * sum(_nbytes(s, d) for s, d in blocks)
    total += sum(_nbytes(s, d) for s, d in single)
    total += sum(_nbytes(s, d) for s, d in scratch)
    total += sum(_nbytes(s, d) for s, d in temps)
    assert total <= VMEM_CAP_BYTES, total
    return total


class _Cast(NamedTuple):
    src: jax.Array
    out_shape: tuple
    block: tuple
    in_index: Callable
    out_index: Callable


def _row_casts(srcs, layer, grid):
    steps = 1
    for g in grid:
        steps *= g

    def linear(*idx):
        step = idx[0]
        for g, i in zip(grid[1:], idx[1:]):
            step = step * g + i
        return step

    casts = []
    for src in srcs:
        _, rows, cols = src.shape
        block_rows = rows // steps
        assert block_rows * steps == rows and block_rows % (2 * SUBLANES) == 0
        casts.append(_Cast(src, (rows, cols), (block_rows, cols),
                           lambda *idx: (layer, linear(*idx), 0),
                           lambda *idx: (linear(*idx), 0)))
    return casts


def _with_casts(body, n_in, n_out, casts):
    n_cast = len(casts)

    def kernel_fn(*refs):
        ins, refs = refs[:n_in], refs[n_in:]
        cast_in, refs = refs[:n_cast], refs[n_cast:]
        outs, refs = refs[:n_out], refs[n_out:]
        cast_out, scratch = refs[:n_cast], refs[n_cast:]
        body(*ins, *outs, *scratch)
        for src, dst in zip(cast_in, cast_out):
            dst[...] = src[...].astype(dst.dtype)

    return (kernel_fn,
            [pl.BlockSpec((None,) + c.block, c.in_index) for c in casts],
            [pl.BlockSpec(c.block, c.out_index) for c in casts],
            [jax.ShapeDtypeStruct(c.out_shape, BF16) for c in casts],
            [c.src for c in casts],
            [(c.block, F32) for c in casts] + [(c.block, BF16) for c in casts])


def _rms(x, g):
    ms = jnp.mean(x * x, axis=-1, keepdims=True)
    return x * lax.rsqrt(ms + EPS) * g


def _softplus(z):
    return jnp.maximum(z, 0.0) + jnp.log(1.0 + jnp.exp(-jnp.abs(z)))


def _prenorm_kernel(x_ref, g_ref, o_ref):
    o_ref[...] = _rms(x_ref[...], g_ref[...]).astype(o_ref.dtype)


def _prenorm(h, g, layer, *, tm=512):
    m, d = h.shape
    return pl.pallas_call(
        _prenorm_kernel,
        grid=(m // tm,),
        in_specs=[pl.BlockSpec((tm, d), lambda i: (i, 0)),
                  pl.BlockSpec((None, 1, d), lambda i: (layer, 0, 0))],
        out_specs=pl.BlockSpec((tm, d), lambda i: (i, 0)),
        out_shape=jax.ShapeDtypeStruct((m, d), BF16),
        compiler_params=pltpu.CompilerParams(
            dimension_semantics=("parallel",),
            vmem_limit_bytes=_vmem_limit([((tm, d), F32), ((tm, d), BF16)],
                                         temps=[((tm, d), F32)] * 2)),
        name="prenorm",
    )(h, g)


def _in_proj_kernel(a_ref, w_ref, o_ref, wb_ref):
    @pl.when(pl.program_id(1) == 0)
    def _():
        wb_ref[...] = w_ref[...].astype(BF16)

    o_ref[...] = jnp.dot(a_ref[...], wb_ref[...],
                         preferred_element_type=F32).astype(o_ref.dtype)


def _in_proj(u, w, layer, *, col_block0, n_out, out_dtype, cast_srcs=(), tm=1024, tn=1024):
    m, k = u.shape
    grid = (n_out // tn, m // tm)
    body, c_in, c_out, c_shapes, c_ops, c_vmem = _with_casts(
        _in_proj_kernel, 2, 1, _row_casts(cast_srcs, layer, grid))
    return pl.pallas_call(
        body,
        grid=grid,
        in_specs=[pl.BlockSpec((tm, k), lambda j, i: (i, 0)),
                  pl.BlockSpec((None, k, tn), lambda j, i: (layer, 0, j + col_block0))] + c_in,
        out_specs=[pl.BlockSpec((tm, tn), lambda j, i: (i, j))] + c_out,
        out_shape=[jax.ShapeDtypeStruct((m, n_out), out_dtype)] + c_shapes,
        scratch_shapes=[pltpu.VMEM((k, tn), BF16)],
        compiler_params=pltpu.CompilerParams(
            dimension_semantics=("parallel", "arbitrary"),
            vmem_limit_bytes=_vmem_limit(
                [((tm, k), BF16), ((k, tn), F32), ((tm, tn), out_dtype)] + c_vmem,
                scratch=[((k, tn), BF16)], temps=[((tm, tn), F32)])),
        name="in_proj",
    )(u, w, *c_ops)


def _attn_kernel(q_ref, k_ref, v_ref, tri_ref, o_ref, carry_ref, *, units):
    tb = ATTN_BLK
    qi = pl.program_id(2)
    q0 = qi * (units * tb)
    tri = tri_ref[...]
    scale = HEAD_DIM ** -0.5
    tt = lax.broadcasted_iota(jnp.int32, (tb, tb), 0)
    ss = lax.broadcasted_iota(jnp.int32, (tb, tb), 1)
    below_diag = ss < tt

    def last_block(x, nblk, fn):
        head, tail = x[:, :(nblk - 1) * tb], fn(x[:, (nblk - 1) * tb:])
        return tail if nblk == 1 else jnp.concatenate([head, tail], axis=1)

    def sweep(starts, blocks, *, diag, carries=None, exists=None):
        rng = range(units)
        kbs = [k_ref[pl.ds(starts[r], blocks[r] * tb), :] for r in rng]
        vbs = [v_ref[pl.ds(starts[r], blocks[r] * tb), :] for r in rng]
        zs = [lax.dot_general(q_ref[r * tb:(r + 1) * tb, :], kbs[r], (((1,), (1,)), ((), ())),
                              preferred_element_type=F32) * scale for r in rng]
        drops, pieces = [], []
        for r in rng:
            nblk = blocks[r]
            drop = _softplus(zs[r])
            if diag:
                drop = last_block(drop, nblk, lambda x: jnp.where(below_diag, x, 0.0))
            if exists is not None:
                drop = jnp.where(exists[r], drop, 0.0)
            hi = drop.astype(BF16)
            lo = (drop - hi.astype(F32)).astype(BF16)
            drops.append(drop)
            pieces.append(jnp.concatenate(
                [jnp.concatenate([hi[:, j * tb:(j + 1) * tb], lo[:, j * tb:(j + 1) * tb]], axis=1)
                 for j in range(nblk)], axis=0))
        sums = [jnp.dot(pieces[r], tri, preferred_element_type=F32) for r in rng]
        ws, new_carries = [], []
        for r in rng:
            nblk = blocks[r]
            running = None if carries is None else carries[r]
            later = [None] * nblk
            for j in reversed(range(nblk)):
                local = sums[r][j * tb:(j + 1) * tb, :tb]
                total = sums[r][j * tb:(j + 1) * tb, tb:]
                later[j] = local if running is None else local + running
                running = total if running is None else running + total
            log_w = zs[r] - drops[r] - (later[0] if nblk == 1
                                        else jnp.concatenate(later, axis=1))
            if diag:
                log_w = last_block(log_w, nblk, lambda x: jnp.where(below_diag, x, -jnp.inf))
            w = jnp.exp(log_w)
            if exists is not None:
                w = jnp.where(exists[r], w, 0.0)
            ws.append(w.astype(BF16))
            new_carries.append(running)
        outs = [jnp.dot(ws[r], vbs[r], preferred_element_type=F32) for r in rng]
        for r in rng:
            rows = slice(r * tb, (r + 1) * tb)
            o_ref[rows, :] = outs[r] if carries is None else o_ref[rows, :] + outs[r]
            carry_ref[r] = new_carries[r]

    @pl.when(qi == 0)
    def _():
        blocks = [min(r + 1, ATTN_FIRST_BLOCKS) for r in range(units)]
        sweep([(r + 1 - n) * tb for r, n in enumerate(blocks)], blocks, diag=True)

    @pl.when(qi > 0)
    def _():
        sweep([pl.multiple_of(q0 + (r + 1 - ATTN_FIRST_BLOCKS) * tb, tb) for r in range(units)],
              [ATTN_FIRST_BLOCKS] * units, diag=True)

    def next_start(r, it):
        return q0 + (r - ATTN_FIRST_BLOCKS - it) * tb

    def pending_min(it):
        m = None
        for r in range(units):
            c = jnp.where(next_start(r, it) >= 0, carry_ref[r], jnp.inf)
            m = c if m is None else jnp.minimum(m, c)
        return jnp.min(m)

    def cond(c):
        _, cmin = c
        return cmin < -LOG_WEIGHT_FLOOR

    def body(c):
        it, _ = c
        starts = [next_start(r, it) for r in range(units)]
        sweep([pl.multiple_of(jnp.maximum(k_lo, 0), tb) for k_lo in starts], [1] * units,
              diag=False, carries=[carry_ref[r] for r in range(units)],
              exists=[k_lo >= 0 for k_lo in starts])
        return it + 1, pending_min(it + 1)

    lax.while_loop(cond, body, (jnp.int32(0), pending_min(0)))


def _attention(qkv, tri, *, heads, cast_srcs=(), cast_layer=0, units=16):
    b, s, _ = qkv.shape
    dh = HEAD_DIM
    tq = units * ATTN_BLK
    grid = (b, heads, s // tq)
    body, c_in, c_out, c_shapes, c_ops, c_vmem = _with_casts(
        functools.partial(_attn_kernel, units=units), 4, 1,
        _row_casts(cast_srcs, cast_layer, grid))
    return pl.pallas_call(
        body,
        grid=grid,
        in_specs=[pl.BlockSpec((None, tq, dh), lambda bi, h, i: (bi, i, h)),
                  pl.BlockSpec((None, s, dh), lambda bi, h, i: (bi, 0, heads + h)),
                  pl.BlockSpec((None, s, dh), lambda bi, h, i: (bi, 0, 2 * heads + h)),
                  pl.BlockSpec(tri.shape, lambda bi, h, i: (0, 0))] + c_in,
        out_specs=[pl.BlockSpec((None, tq, dh), lambda bi, h, i: (bi, i, h))] + c_out,
        out_shape=[jax.ShapeDtypeStruct((b, s, heads * dh), F32)] + c_shapes,
        scratch_shapes=[pltpu.VMEM((units, ATTN_BLK, ATTN_BLK), F32)],
        compiler_params=pltpu.CompilerParams(
            dimension_semantics=("parallel", "parallel", "parallel"),
            vmem_limit_bytes=_vmem_limit(
                [((tq, dh), BF16), ((s, dh), BF16), ((s, dh), BF16),
                 (tri.shape, BF16), ((tq, dh), F32)] + c_vmem,
                scratch=[((units, ATTN_BLK, ATTN_BLK), F32)],
                temps=[((ATTN_BLK, ATTN_FIRST_BLOCKS * ATTN_BLK), F32)] * 8 * units)),
        name="stickbreak_attn",
    )(qkv, qkv, qkv, tri, *c_ops)


def _mixer_kernel(cv_ref, cg_ref, rx_ref, ry_ref, dww_ref, pv_ref, wai_ref, o_ref,
                  ush, rxbuf, hcarry, wtile, *, ts, ch):
    s = pl.program_id(1)

    @pl.when(s == 0)
    def _():
        ush[0, 0:CONV_HALO, :] = jnp.zeros((CONV_HALO, ch), F32)
        rxbuf[0:LRU_HALO, :] = jnp.zeros((LRU_HALO, ch), F32)
        hcarry[...] = jnp.zeros_like(hcarry)
        for k in range(DW_CONV_LEN):
            wtile[k] = jnp.broadcast_to(dww_ref[k:k + 1, :], (SUBLANES, ch))

    ush[0, CONV_HALO:CONV_HALO + ts, :] = cv_ref[...] * jax.nn.sigmoid(cg_ref[...])
    u_all = ush[0]
    for p in range(1, SUBLANES):
        ush[p] = pltpu.roll(u_all, p, 0)
    base = CONV_HALO - (DW_CONV_LEN - 1)
    for rt in range(ts // CONV_ROWS):
        r0 = rt * CONV_ROWS
        acc = jnp.broadcast_to(pv_ref[0:1, :], (CONV_ROWS, ch))
        for k in range(DW_CONV_LEN):
            p = -(base + k) % SUBLANES
            lo = base + k + p + r0
            wk = jnp.concatenate([wtile[k]] * (CONV_ROWS // SUBLANES), axis=0)
            acc = acc + wk * ush[p, lo:lo + CONV_ROWS, :]
        mu = jnp.mean(acc, axis=-1, keepdims=True)
        xc = acc - mu
        var = jnp.mean(xc * xc, axis=-1, keepdims=True)
        y = xc * lax.rsqrt(var + EPS) * pv_ref[1:2, :] + pv_ref[2:3, :]
        y = y * jax.nn.sigmoid(y)
        o_ref[r0:r0 + CONV_ROWS, 0:ch] = _rms(y, pv_ref[3:4, :]).astype(o_ref.dtype)
    ush[0, 0:CONV_HALO, :] = ush[0, ts:ts + CONV_HALO, :]

    rxbuf[LRU_HALO:LRU_HALO + ts, :] = rx_ref[...]
    base = LRU_HALO - (LRU_CONV_LEN - 1)
    xr = jnp.broadcast_to(pv_ref[4:5, :], (ts, ch))
    for k in range(LRU_CONV_LEN):
        xr = xr + pv_ref[9 + k:10 + k, :] * rxbuf[base + k:base + k + ts, :]
    rxbuf[0:LRU_HALO, :] = rxbuf[ts:ts + LRU_HALO, :]
    xb = xr.astype(BF16)
    bd = ch // LRU_BLOCKS
    gates = [jnp.dot(xb[:, n * bd:(n + 1) * bd], wai_ref[n], preferred_element_type=F32)
             for n in range(LRU_BLOCKS)]
    gate_a = jnp.concatenate([g[:, :bd] for g in gates], axis=1) + pv_ref[5:6, :]
    gate_i = jnp.concatenate([g[:, bd:] for g in gates], axis=1) + pv_ref[6:7, :]
    r = jax.nn.sigmoid(gate_a)
    i = jax.nn.sigmoid(gate_i)
    log_a = -LRU_C * r * _softplus(-pv_ref[7:8, :])
    a = jnp.exp(log_a)
    b = jnp.sqrt(-jnp.tanh(log_a) * (a * a + 1.0)) * (i * xr)
    groups = ts // SUBLANES
    a = a.reshape(groups, SUBLANES, ch)
    b = b.reshape(groups, SUBLANES, ch)
    sub = lax.broadcasted_iota(jnp.int32, (groups, SUBLANES, ch), 1)
    d = 1
    while d < SUBLANES:
        a_prev = jnp.where(sub >= d, pltpu.roll(a, d, 1), 1.0)
        b_prev = jnp.where(sub >= d, pltpu.roll(b, d, 1), 0.0)
        b = a * b_prev + b
        a = a * a_prev
        d *= 2
    h_prev = hcarry[...]
    hs = []
    for gi in range(groups):
        hg = b[gi] + a[gi] * h_prev
        hs.append(hg)
        h_prev = jnp.broadcast_to(hg[SUBLANES - 1:SUBLANES, :], (SUBLANES, ch))
    hcarry[...] = h_prev
    h = jnp.concatenate(hs, axis=0)
    y = h * jax.nn.gelu(ry_ref[...])
    o_ref[:, ch:2 * ch] = _rms(y, pv_ref[8:9, :]).astype(o_ref.dtype)


def _mixers(rest, dww, pv, wai, layer, *, cast_srcs=(), ts=512):
    b, s, four_ch = rest.shape
    ch = four_ch // 4
    bd = ch // LRU_BLOCKS
    assert ts % CONV_ROWS == 0 and CONV_HALO % SUBLANES == 0 and CONV_HALO >= DW_CONV_LEN
    blk = lambda c: pl.BlockSpec((None, ts, ch), lambda bi, si: (bi, si, c))
    grid = (b, s // ts)
    body, c_in, c_out, c_shapes, c_ops, c_vmem = _with_casts(
        functools.partial(_mixer_kernel, ts=ts, ch=ch), 7, 1, _row_casts(cast_srcs, layer, grid))
    return pl.pallas_call(
        body,
        grid=grid,
        in_specs=[blk(0), blk(1), blk(2), blk(3),
                  pl.BlockSpec((None,) + dww.shape[1:], lambda bi, si: (layer, 0, 0)),
                  pl.BlockSpec((None,) + pv.shape[1:], lambda bi, si: (layer, 0, 0)),
                  pl.BlockSpec((None, LRU_BLOCKS, bd, 2 * bd),
                               lambda bi, si: (layer, 0, 0, 0))] + c_in,
        out_specs=[pl.BlockSpec((None, ts, 2 * ch), lambda bi, si: (bi, si, 0))] + c_out,
        out_shape=[jax.ShapeDtypeStruct((b, s, 2 * ch), BF16)] + c_shapes,
        scratch_shapes=[pltpu.VMEM((SUBLANES, CONV_HALO + ts, ch), F32),
                        pltpu.VMEM((LRU_HALO + ts, ch), F32),
                        pltpu.VMEM((SUBLANES, ch), F32),
                        pltpu.VMEM((DW_CONV_LEN, SUBLANES, ch), F32)],
        compiler_params=pltpu.CompilerParams(
            dimension_semantics=("parallel", "arbitrary"),
            vmem_limit_bytes=_vmem_limit(
                [((ts, ch), F32)] * 4 + [(dww.shape[1:], F32), (pv.shape[1:], F32),
                                         ((LRU_BLOCKS, bd, 2 * bd), BF16), ((ts, 2 * ch), BF16)]
                + c_vmem,
                scratch=[((SUBLANES, CONV_HALO + ts, ch), F32), ((LRU_HALO + ts, ch), F32),
                         ((DW_CONV_LEN, SUBLANES, ch), F32)],
                temps=[((ts, ch), F32)] * 16)),
        name="seq_mixers",
    )(rest, rest, rest, rest, dww, pv, wai, *c_ops)


def _out_proj_kernel(ya_ref, mr_ref, h_ref, w1_ref, w2_ref, ga_ref, gpost_ref, gffn_ref,
                     hout_ref, u_ref):
    tm = ya_ref.shape[0]
    groups = [slice(p * tm // OUT_PROJ_GROUPS, (p + 1) * tm // OUT_PROJ_GROUPS)
              for p in range(OUT_PROJ_GROUPS)]
    nas = [_rms(ya_ref[r, :], ga_ref[...]).astype(BF16) for r in groups]
    outs = [jnp.dot(na, w1_ref[...], preferred_element_type=F32)
            + jnp.dot(mr_ref[r, :], w2_ref[...], preferred_element_type=F32)
            for na, r in zip(nas, groups)]
    for o, r in zip(outs, groups):
        hn = h_ref[r, :] + _rms(o, gpost_ref[...])
        hout_ref[r, :] = hn
        u_ref[r, :] = _rms(hn, gffn_ref[...]).astype(u_ref.dtype)


def _out_proj(ya, mr, h, w_out, g_attn, g_post, g_ffn, layer, *, tm=512):
    m, d = h.shape
    ka = ya.shape[1]
    kr = mr.shape[1]
    assert ka == kr and w_out.shape == (ka + kr, d)
    gain = lambda n: pl.BlockSpec((None, 1, n), lambda i: (layer, 0, 0))
    weight = lambda rows, blk: pl.BlockSpec((rows, d), lambda i: (blk, 0),
                                            pipeline_mode=pl.Buffered(1))
    return pl.pallas_call(
        _out_proj_kernel,
        grid=(m // tm,),
        in_specs=[pl.BlockSpec((tm, ka), lambda i: (i, 0)),
                  pl.BlockSpec((tm, kr), lambda i: (i, 0)),
                  pl.BlockSpec((tm, d), lambda i: (i, 0)),
                  weight(ka, 0), weight(kr, 1),
                  gain(ka), gain(d), gain(d)],
        out_specs=[pl.BlockSpec((tm, d), lambda i: (i, 0)),
                   pl.BlockSpec((tm, d), lambda i: (i, 0))],
        out_shape=[jax.ShapeDtypeStruct((m, d), F32), jax.ShapeDtypeStruct((m, d), BF16)],
        compiler_params=pltpu.CompilerParams(
            dimension_semantics=("parallel",),
            vmem_limit_bytes=_vmem_limit(
                [((tm, ka), F32), ((tm, kr), BF16), ((tm, d), F32),
                 ((tm, d), F32), ((tm, d), BF16)],
                single=[((ka, d), BF16), ((kr, d), BF16)],
                temps=[((tm, d), F32)] * 3)),
        name="out_proj",
    )(ya, mr, h, w_out, w_out, g_attn, g_post, g_ffn)


def _ffn_kernel(u_ref, wg_ref, wu_ref, wd_ref, h_ref, gpost_ref, *rest, emit_next):
    if emit_next:
        gnext_ref, hout_ref, unext_ref, acc_ref = rest
    else:
        hout_ref, acc_ref = rest
    f = pl.program_id(1)

    @pl.when(f == 0)
    def _():
        acc_ref[...] = jnp.zeros_like(acc_ref)

    u = u_ref[...]
    g = jnp.dot(u, wg_ref[...], preferred_element_type=F32)
    up = jnp.dot(u, wu_ref[...], preferred_element_type=F32)
    act = (g * jax.nn.sigmoid(g) * up).astype(BF16)
    acc_ref[...] += jnp.dot(act, wd_ref[...], preferred_element_type=F32)

    @pl.when(f == pl.num_programs(1) - 1)
    def _():
        hn = h_ref[...] + _rms(acc_ref[...], gpost_ref[...])
        hout_ref[...] = hn
        if emit_next:
            unext_ref[...] = _rms(hn, gnext_ref[...]).astype(unext_ref.dtype)


def _ffn(u, h, wg, wu, wd, g_post, g_next, layer, *, tm=512, tf=512):
    m, d = h.shape
    dff = wg.shape[1]
    emit_next = g_next is not None
    operands = [u, wg, wu, wd, h, g_post]
    in_specs = [pl.BlockSpec((tm, d), lambda i, f: (i, 0)),
                pl.BlockSpec((d, tf), lambda i, f: (0, f)),
                pl.BlockSpec((d, tf), lambda i, f: (0, f)),
                pl.BlockSpec((tf, d), lambda i, f: (f, 0)),
                pl.BlockSpec((tm, d), lambda i, f: (i, 0)),
                pl.BlockSpec((None, 1, d), lambda i, f: (layer, 0, 0))]
    out_specs = [pl.BlockSpec((tm, d), lambda i, f: (i, 0))]
    out_shape = [jax.ShapeDtypeStruct((m, d), F32)]
    blocks = [((tm, d), BF16), ((d, tf), BF16), ((d, tf), BF16), ((tf, d), BF16),
              ((tm, d), F32), ((tm, d), F32)]
    if emit_next:
        operands.append(g_next)
        in_specs.append(pl.BlockSpec((None, 1, d), lambda i, f: (layer + 1, 0, 0)))
        out_specs.append(pl.BlockSpec((tm, d), lambda i, f: (i, 0)))
        out_shape.append(jax.ShapeDtypeStruct((m, d), BF16))
        blocks.append(((tm, d), BF16))
    return pl.pallas_call(
        functools.partial(_ffn_kernel, emit_next=emit_next),
        grid=(m // tm, dff // tf),
        in_specs=in_specs,
        out_specs=out_specs,
        out_shape=out_shape,
        scratch_shapes=[pltpu.VMEM((tm, d), F32)],
        compiler_params=pltpu.CompilerParams(
            dimension_semantics=("parallel", "arbitrary"),
            vmem_limit_bytes=_vmem_limit(blocks, scratch=[((tm, d), F32)],
                                         temps=[((tm, tf), F32)] * 3 + [((tm, d), F32)] * 2)),
        name="swiglu_ffn",
    )(*operands)


def kernel(x, w_in, w_out, g_pre_mix, g_post_mix, g_pre_ffn, g_post_ffn, g_attn_grp,
           g_conv_grp, g_lru_grp, dw_conv_w, dw_conv_b, conv_ln_g, conv_ln_b, lru_conv_w,
           lru_conv_b, lru_w_a, lru_b_a, lru_w_i, lru_b_i, lru_lambda, w_gate, w_up, w_down):
    bsz, s_len, d = x.shape
    depth = w_in.shape[0]
    attn_w = g_attn_grp.shape[1]
    ch = g_conv_grp.shape[1]
    heads = attn_w // HEAD_DIM
    m = bsz * s_len
    assert w_in.shape[2] == 3 * attn_w + 4 * ch and lru_w_a.shape[1] == LRU_BLOCKS

    wai = jnp.concatenate([lru_w_a, lru_w_i], axis=-1).astype(BF16)

    row3 = lambda v: v[:, None, :]
    pv = jnp.concatenate(
        [row3(dw_conv_b), row3(conv_ln_g), row3(conv_ln_b), row3(g_conv_grp), row3(lru_conv_b),
         row3(lru_b_a), row3(lru_b_i), row3(lru_lambda), row3(g_lru_grp), lru_conv_w,
         jnp.zeros((depth, 3, ch), F32)], axis=1)
    dww = jnp.concatenate([dw_conv_w, jnp.zeros((depth, 1, ch), F32)], axis=1)
    g_pre_mix3, g_post_mix3 = row3(g_pre_mix), row3(g_post_mix)
    g_pre_ffn3, g_post_ffn3 = row3(g_pre_ffn), row3(g_post_ffn)
    g_attn3 = row3(g_attn_grp)

    jj = lax.broadcasted_iota(jnp.int32, (ATTN_BLK, ATTN_BLK), 0)
    ss = lax.broadcasted_iota(jnp.int32, (ATTN_BLK, ATTN_BLK), 1)
    tri = jnp.concatenate([(jj > ss).astype(BF16), jnp.ones((ATTN_BLK, ATTN_BLK), BF16)], axis=1)
    tri = jnp.concatenate([tri, tri], axis=0)

    tn = 1024
    h = x.reshape(m, d)
    u = _prenorm(h, g_pre_mix3, 0)
    for l in range(depth):
        (qkv,) = _in_proj(u, w_in, l, col_block0=0, n_out=3 * attn_w, out_dtype=BF16, tn=tn)
        rest, w_out_b = _in_proj(u, w_in, l, col_block0=3 * attn_w // tn, n_out=4 * ch,
                                 out_dtype=F32, cast_srcs=(w_out,), tn=tn)
        ya, w_gate_b, w_up_b = _attention(qkv.reshape(bsz, s_len, 3 * attn_w), tri, heads=heads,
                                          cast_srcs=(w_gate, w_up), cast_layer=l)
        mr, w_down_b = _mixers(rest.reshape(bsz, s_len, 4 * ch), dww, pv, wai, l,
                               cast_srcs=(w_down,))
        h, u2 = _out_proj(ya.reshape(m, attn_w), mr.reshape(m, 2 * ch), h, w_out_b,
                          g_attn3, g_post_mix3, g_pre_ffn3, l)
        if l == depth - 1:
            (h,) = _ffn(u2, h, w_gate_b, w_up_b, w_down_b, g_post_ffn3, None, l)
        else:
            h, u = _ffn(u2, h, w_gate_b, w_up_b, w_down_b, g_post_ffn3, g_pre_mix3, l)
    return h.reshape(bsz, s_len, d)
```

```python
import functools
from typing import Callable, NamedTuple

import jax
import jax.numpy as jnp
from jax import lax
from jax.experimental import pallas as pl
from jax.experimental.pallas import tpu as pltpu

F32 = jnp.float32
BF16 = jnp.bfloat16

EPS = 1e-6
HEAD_DIM = 128
DW_CONV_LEN = 31
LRU_CONV_LEN = 4
LRU_BLOCKS = 4
LRU_C = 8.0

V7X_VMEM_BYTES = 64 * 1024 * 1024
VMEM_CAP_BYTES = V7X_VMEM_BYTES - 8 * 1024 * 1024
SUBLANES = 8
LANES = 128

LOG_WEIGHT_FLOOR = -104.0

ATTN_BLK = LANES
ATTN_FIRST_BLOCKS = 3

CONV_HALO = 32
LRU_HALO = 8
CONV_ROWS = 64
OUT_PROJ_GROUPS = 2
FFN_EPILOGUE_ROWS = 128


def _nbytes(shape, dtype):
    n = 1
    for s in shape:
        n *= s
    return n * jnp.dtype(dtype).itemsize


def _vmem_limit(blocks, scratch=(), temps=(), single=()):
    total = 2 * sum(_nbytes(s, d) for s, d in blocks)
    total += sum(_nbytes(s, d) for s, d in single)
    total += sum(_nbytes(s, d) for s, d in scratch)
    total += sum(_nbytes(s, d) for s, d in temps)
    assert total <= VMEM_CAP_BYTES, total
    return total


class _Cast(NamedTuple):
    src: jax.Array
    out_shape: tuple
    block: tuple
    in_index: Callable
    out_index: Callable


def _row_casts(srcs, layer, grid):
    steps = 1
    for g in grid:
        steps *= g

    def linear(*idx):
        step = idx[0]
        for g, i in zip(grid[1:], idx[1:]):
            step = step * g + i
        return step

    casts = []
    for src in srcs:
        _, rows, cols = src.shape
        block_rows = rows // steps
        assert block_rows * steps == rows and block_rows % (2 * SUBLANES) == 0
        casts.append(_Cast(src, (rows, cols), (block_rows, cols),
                           lambda *idx: (layer, linear(*idx), 0),
                           lambda *idx: (linear(*idx), 0)))
    return casts


def _with_casts(body, n_in, n_out, casts):
    n_cast = len(casts)

    def kernel_fn(*refs):
        ins, refs = refs[:n_in], refs[n_in:]
        cast_in, refs = refs[:n_cast], refs[n_cast:]
        outs, refs = refs[:n_out], refs[n_out:]
        cast_out, scratch = refs[:n_cast], refs[n_cast:]
        body(*ins, *outs, *scratch)
        for src, dst in zip(cast_in, cast_out):
            dst[...] = src[...].astype(dst.dtype)

    return (kernel_fn,
            [pl.BlockSpec((None,) + c.block, c.in_index) for c in casts],
            [pl.BlockSpec(c.block, c.out_index) for c in casts],
            [jax.ShapeDtypeStruct(c.out_shape, BF16) for c in casts],
            [c.src for c in casts],
            [(c.block, F32) for c in casts] + [(c.block, BF16) for c in casts])


def _rms(x, g):
    ms = jnp.mean(x * x, axis=-1, keepdims=True)
    return x * lax.rsqrt(ms + EPS) * g


def _softplus(z):
    return jnp.maximum(z, 0.0) + jnp.log(1.0 + jnp.exp(-jnp.abs(z)))


def _prenorm_kernel(x_ref, g_ref, o_ref):
    o_ref[...] = _rms(x_ref[...], g_ref[...]).astype(o_ref.dtype)


def _prenorm(h, g, layer, *, tm=512):
    m, d = h.shape
    return pl.pallas_call(
        _prenorm_kernel,
        grid=(m // tm,),
        in_specs=[pl.BlockSpec((tm, d), lambda i: (i, 0)),
                  pl.BlockSpec((None, 1, d), lambda i: (layer, 0, 0))],
        out_specs=pl.BlockSpec((tm, d), lambda i: (i, 0)),
        out_shape=jax.ShapeDtypeStruct((m, d), BF16),
        compiler_params=pltpu.CompilerParams(
            dimension_semantics=("parallel",),
            vmem_limit_bytes=_vmem_limit([((tm, d), F32), ((tm, d), BF16)],
                                         temps=[((tm, d), F32)] * 2)),
        name="prenorm",
    )(h, g)


def _in_proj_kernel(a_ref, w_ref, o_ref, wb_ref):
    @pl.when(pl.program_id(1) == 0)
    def _():
        wb_ref[...] = w_ref[...].astype(BF16)

    o_ref[...] = jnp.dot(a_ref[...], wb_ref[...],
                         preferred_element_type=F32).astype(o_ref.dtype)


def _in_proj(u, w, layer, *, col_block0, n_out, out_dtype, cast_srcs=(), tm=1024, tn=1024):
    m, k = u.shape
    grid = (n_out // tn, m // tm)
    body, c_in, c_out, c_shapes, c_ops, c_vmem = _with_casts(
        _in_proj_kernel, 2, 1, _row_casts(cast_srcs, layer, grid))
    return pl.pallas_call(
        body,
        grid=grid,
        in_specs=[pl.BlockSpec((tm, k), lambda j, i: (i, 0)),
                  pl.BlockSpec((None, k, tn), lambda j, i: (layer, 0, j + col_block0))] + c_in,
        out_specs=[pl.BlockSpec((tm, tn), lambda j, i: (i, j))] + c_out,
        out_shape=[jax.ShapeDtypeStruct((m, n_out), out_dtype)] + c_shapes,
        scratch_shapes=[pltpu.VMEM((k, tn), BF16)],
        compiler_params=pltpu.CompilerParams(
            dimension_semantics=("parallel", "arbitrary"),
            vmem_limit_bytes=_vmem_limit(
                [((tm, k), BF16), ((k, tn), F32), ((tm, tn), out_dtype)] + c_vmem,
                scratch=[((k, tn), BF16)], temps=[((tm, tn), F32)])),
        name="in_proj",
    )(u, w, *c_ops)


def _attn_kernel(q_ref, k_ref, v_ref, tri_ref, o_ref, carry_ref, *, units):
    tb = ATTN_BLK
    qi = pl.program_id(2)
    q0 = qi * (units * tb)
    tri = tri_ref[...]
    scale = HEAD_DIM ** -0.5
    tt = lax.broadcasted_iota(jnp.int32, (tb, tb), 0)
    ss = lax.broadcasted_iota(jnp.int32, (tb, tb), 1)
    below_diag = ss < tt

    def last_block(x, nblk, fn):
        head, tail = x[:, :(nblk - 1) * tb], fn(x[:, (nblk - 1) * tb:])
        return tail if nblk == 1 else jnp.concatenate([head, tail], axis=1)

    def sweep(starts, blocks, *, diag, carries=None, exists=None):
        rng = range(units)
        kbs = [k_ref[pl.ds(starts[r], blocks[r] * tb), :] for r in rng]
        vbs = [v_ref[pl.ds(starts[r], blocks[r] * tb), :] for r in rng]
        zs = [lax.dot_general(q_ref[r * tb:(r + 1) * tb, :], kbs[r], (((1,), (1,)), ((), ())),
                              preferred_element_type=F32) * scale for r in rng]
        drops, pieces = [], []
        for r in rng:
            nblk = blocks[r]
            drop = _softplus(zs[r])
            if diag:
                drop = last_block(drop, nblk, lambda x: jnp.where(below_diag, x, 0.0))
            if exists is not None:
                drop = jnp.where(exists[r], drop, 0.0)
            hi = drop.astype(BF16)
            lo = (drop - hi.astype(F32)).astype(BF16)
            drops.append(drop)
            pieces.append(jnp.concatenate(
                [jnp.concatenate([hi[:, j * tb:(j + 1) * tb], lo[:, j * tb:(j + 1) * tb]], axis=1)
                 for j in range(nblk)], axis=0))
        sums = [jnp.dot(pieces[r], tri, preferred_element_type=F32) for r in rng]
        ws, new_carries = [], []
        for r in rng:
            nblk = blocks[r]
            running = None if carries is None else carries[r]
            later = [None] * nblk
            for j in reversed(range(nblk)):
                local = sums[r][j * tb:(j + 1) * tb, :tb]
                total = sums[r][j * tb:(j + 1) * tb, tb:]
                later[j] = local if running is None else local + running
                running = total if running is None else running + total
            log_w = zs[r] - drops[r] - (later[0] if nblk == 1
                                        else jnp.concatenate(later, axis=1))
            if diag:
                log_w = last_block(log_w, nblk, lambda x: jnp.where(below_diag, x, -jnp.inf))
            w = jnp.exp(log_w)
            if exists is not None:
                w = jnp.where(exists[r], w, 0.0)
            ws.append(w.astype(BF16))
            new_carries.append(running)
        outs = [jnp.dot(ws[r], vbs[r], preferred_element_type=F32) for r in rng]
        for r in rng:
            rows = slice(r * tb, (r + 1) * tb)
            o_ref[rows, :] = outs[r] if carries is None else o_ref[rows, :] + outs[r]
            carry_ref[r] = new_carries[r]

    @pl.when(qi == 0)
    def _():
        blocks = [min(r + 1, ATTN_FIRST_BLOCKS) for r in range(units)]
        sweep([(r + 1 - n) * tb for r, n in enumerate(blocks)], blocks, diag=True)

    @pl.when(qi > 0)
    def _():
        sweep([pl.multiple_of(q0 + (r + 1 - ATTN_FIRST_BLOCKS) * tb, tb) for r in range(units)],
              [ATTN_FIRST_BLOCKS] * units, diag=True)

    def next_start(r, it):
        return q0 + (r - ATTN_FIRST_BLOCKS - it) * tb

    def pending_min(it):
        m = None
        for r in range(units):
            c = jnp.where(next_start(r, it) >= 0, carry_ref[r], jnp.inf)
            m = c if m is None else jnp.minimum(m, c)
        return jnp.min(m)

    def cond(c):
        _, cmin = c
        return cmin < -LOG_WEIGHT_FLOOR

    def body(c):
        it, _ = c
        starts = [next_start(r, it) for r in range(units)]
        sweep([pl.multiple_of(jnp.maximum(k_lo, 0), tb) for k_lo in starts], [1] * units,
              diag=False, carries=[carry_ref[r] for r in range(units)],
              exists=[k_lo >= 0 for k_lo in starts])
        return it + 1, pending_min(it + 1)

    lax.while_loop(cond, body, (jnp.int32(0), pending_min(0)))


def _attention(qkv, tri, *, heads, cast_srcs=(), cast_layer=0, units=16):
    b, s, _ = qkv.shape
    dh = HEAD_DIM
    tq = units * ATTN_BLK
    grid = (b, heads, s // tq)
    body, c_in, c_out, c_shapes, c_ops, c_vmem = _with_casts(
        functools.partial(_attn_kernel, units=units), 4, 1,
        _row_casts(cast_srcs, cast_layer, grid))
    return pl.pallas_call(
        body,
        grid=grid,
        in_specs=[pl.BlockSpec((None, tq, dh), lambda bi, h, i: (bi, i, h)),
                  pl.BlockSpec((None, s, dh), lambda bi, h, i: (bi, 0, heads + h)),
                  pl.BlockSpec((None, s, dh), lambda bi, h, i: (bi, 0, 2 * heads + h)),
                  pl.BlockSpec(tri.shape, lambda bi, h, i: (0, 0))] + c_in,
        out_specs=[pl.BlockSpec((None, tq, dh), lambda bi, h, i: (bi, i, h))] + c_out,
        out_shape=[jax.ShapeDtypeStruct((b, s, heads * dh), F32)] + c_shapes,
        scratch_shapes=[pltpu.VMEM((units, ATTN_BLK, ATTN_BLK), F32)],
        compiler_params=pltpu.CompilerParams(
            dimension_semantics=("parallel", "parallel", "parallel"),
            vmem_limit_bytes=_vmem_limit(
                [((tq, dh), BF16), ((s, dh), BF16), ((s, dh), BF16),
                 (tri.shape, BF16), ((tq, dh), F32)] + c_vmem,
                scratch=[((units, ATTN_BLK, ATTN_BLK), F32)],
                temps=[((ATTN_BLK, ATTN_FIRST_BLOCKS * ATTN_BLK), F32)] * 8 * units)),
        name="stickbreak_attn",
    )(qkv, qkv, qkv, tri, *c_ops)


def _mixer_kernel(cv_ref, cg_ref, rx_ref, ry_ref, dww_ref, pv_ref, wai_ref, o_ref,
                  ush, rxbuf, hcarry, wtile, *, ts, ch):
    s = pl.program_id(1)

    @pl.when(s == 0)
    def _():
        ush[0, 0:CONV_HALO, :] = jnp.zeros((CONV_HALO, ch), F32)
        rxbuf[0:LRU_HALO, :] = jnp.zeros((LRU_HALO, ch), F32)
        hcarry[...] = jnp.zeros_like(hcarry)
        for k in range(DW_CONV_LEN):
            wtile[k] = jnp.broadcast_to(dww_ref[k:k + 1, :], (SUBLANES, ch))

    ush[0, CONV_HALO:CONV_HALO + ts, :] = cv_ref[...] * jax.nn.sigmoid(cg_ref[...])
    u_all = ush[0]
    for p in range(1, SUBLANES):
        ush[p] = pltpu.roll(u_all, p, 0)
    base = CONV_HALO - (DW_CONV_LEN - 1)
    for rt in range(ts // CONV_ROWS):
        r0 = rt * CONV_ROWS
        acc = jnp.broadcast_to(pv_ref[0:1, :], (CONV_ROWS, ch))
        for k in range(DW_CONV_LEN):
            p = -(base + k) % SUBLANES
            lo = base + k + p + r0
            wk = jnp.concatenate([wtile[k]] * (CONV_ROWS // SUBLANES), axis=0)
            acc = acc + wk * ush[p, lo:lo + CONV_ROWS, :]
        mu = jnp.mean(acc, axis=-1, keepdims=True)
        xc = acc - mu
        var = jnp.mean(xc * xc, axis=-1, keepdims=True)
        y = xc * lax.rsqrt(var + EPS) * pv_ref[1:2, :] + pv_ref[2:3, :]
        y = y * jax.nn.sigmoid(y)
        o_ref[r0:r0 + CONV_ROWS, 0:ch] = _rms(y, pv_ref[3:4, :]).astype(o_ref.dtype)
    ush[0, 0:CONV_HALO, :] = ush[0, ts:ts + CONV_HALO, :]

    rxbuf[LRU_HALO:LRU_HALO + ts, :] = rx_ref[...]
    base = LRU_HALO - (LRU_CONV_LEN - 1)
    xr = jnp.broadcast_to(pv_ref[4:5, :], (ts, ch))
    for k in range(LRU_CONV_LEN):
        xr = xr + pv_ref[9 + k:10 + k, :] * rxbuf[base + k:base + k + ts, :]
    rxbuf[0:LRU_HALO, :] = rxbuf[ts:ts + LRU_HALO, :]
    xb = xr.astype(BF16)
    bd = ch // LRU_BLOCKS
    gates = [jnp.dot(xb[:, n * bd:(n + 1) * bd], wai_ref[n], preferred_element_type=F32)
             for n in range(LRU_BLOCKS)]
    gate_a = jnp.concatenate([g[:, :bd] for g in gates], axis=1) + pv_ref[5:6, :]
    gate_i = jnp.concatenate([g[:, bd:] for g in gates], axis=1) + pv_ref[6:7, :]
    r = jax.nn.sigmoid(gate_a)
    i = jax.nn.sigmoid(gate_i)
    log_a = -LRU_C * r * _softplus(-pv_ref[7:8, :])
    a = jnp.exp(log_a)
    b = jnp.sqrt(-jnp.tanh(log_a) * (a * a + 1.0)) * (i * xr)
    groups = ts // SUBLANES
    a = a.reshape(groups, SUBLANES, ch)
    b = b.reshape(groups, SUBLANES, ch)
    sub = lax.broadcasted_iota(jnp.int32, (groups, SUBLANES, ch), 1)
    d = 1
    while d < SUBLANES:
        a_prev = jnp.where(sub >= d, pltpu.roll(a, d, 1), 1.0)
        b_prev = jnp.where(sub >= d, pltpu.roll(b, d, 1), 0.0)
        b = a * b_prev + b
        a = a * a_prev
        d *= 2
    h_prev = hcarry[...]
    hs = []
    for gi in range(groups):
        hg = b[gi] + a[gi] * h_prev
        hs.append(hg)
        h_prev = jnp.broadcast_to(hg[SUBLANES - 1:SUBLANES, :], (SUBLANES, ch))
    hcarry[...] = h_prev
    h = jnp.concatenate(hs, axis=0)
    y = h * jax.nn.gelu(ry_ref[...])
    o_ref[:, ch:2 * ch] = _rms(y, pv_ref[8:9, :]).astype(o_ref.dtype)


def _mixers(rest, dww, pv, wai, layer, *, cast_srcs=(), ts=512):
    b, s, four_ch = rest.shape
    ch = four_ch // 4
    bd = ch // LRU_BLOCKS
    assert ts % CONV_ROWS == 0 and CONV_HALO % SUBLANES == 0 and CONV_HALO >= DW_CONV_LEN
    blk = lambda c: pl.BlockSpec((None, ts, ch), lambda bi, si: (bi, si, c))
    grid = (b, s // ts)
    body, c_in, c_out, c_shapes, c_ops, c_vmem = _with_casts(
        functools.partial(_mixer_kernel, ts=ts, ch=ch), 7, 1, _row_casts(cast_srcs, layer, grid))
    return pl.pallas_call(
        body,
        grid=grid,
        in_specs=[blk(0), blk(1), blk(2), blk(3),
                  pl.BlockSpec((None,) + dww.shape[1:], lambda bi, si: (layer, 0, 0)),
                  pl.BlockSpec((None,) + pv.shape[1:], lambda bi, si: (layer, 0, 0)),
                  pl.BlockSpec((None, LRU_BLOCKS, bd, 2 * bd),
                               lambda bi, si: (layer, 0, 0, 0))] + c_in,
        out_specs=[pl.BlockSpec((None, ts, 2 * ch), lambda bi, si: (bi, si, 0))] + c_out,
        out_shape=[jax.ShapeDtypeStruct((b, s, 2 * ch), BF16)] + c_shapes,
        scratch_shapes=[pltpu.VMEM((SUBLANES, CONV_HALO + ts, ch), F32),
                        pltpu.VMEM((LRU_HALO + ts, ch), F32),
                        pltpu.VMEM((SUBLANES, ch), F32),
                        pltpu.VMEM((DW_CONV_LEN, SUBLANES, ch), F32)],
        compiler_params=pltpu.CompilerParams(
            dimension_semantics=("parallel", "arbitrary"),
            vmem_limit_bytes=_vmem_limit(
                [((ts, ch), F32)] * 4 + [(dww.shape[1:], F32), (pv.shape[1:], F32),
                                         ((LRU_BLOCKS, bd, 2 * bd), BF16), ((ts, 2 * ch), BF16)]
                + c_vmem,
                scratch=[((SUBLANES, CONV_HALO + ts, ch), F32), ((LRU_HALO + ts, ch), F32),
                         ((DW_CONV_LEN, SUBLANES, ch), F32)],
                temps=[((ts, ch), F32)] * 16)),
        name="seq_mixers",
    )(rest, rest, rest, rest, dww, pv, wai, *c_ops)


def _out_proj_kernel(ya_ref, mr_ref, h_ref, w1_ref, w2_ref, ga_ref, gpost_ref, gffn_ref,
                     hout_ref, u_ref):
    tm = ya_ref.shape[0]
    groups = [slice(p * tm // OUT_PROJ_GROUPS, (p + 1) * tm // OUT_PROJ_GROUPS)
              for p in range(OUT_PROJ_GROUPS)]
    nas = [_rms(ya_ref[r, :], ga_ref[...]).astype(BF16) for r in groups]
    outs = [jnp.dot(na, w1_ref[...], preferred_element_type=F32)
            + jnp.dot(mr_ref[r, :], w2_ref[...], preferred_element_type=F32)
            for na, r in zip(nas, groups)]
    for o, r in zip(outs, groups):
        hn = h_ref[r, :] + _rms(o, gpost_ref[...])
        hout_ref[r, :] = hn
        u_ref[r, :] = _rms(hn, gffn_ref[...]).astype(u_ref.dtype)


def _out_proj(ya, mr, h, w_out, g_attn, g_post, g_ffn, layer, *, tm=512):
    m, d = h.shape
    ka = ya.shape[1]
    kr = mr.shape[1]
    assert ka == kr and w_out.shape == (ka + kr, d)
    gain = lambda n: pl.BlockSpec((None, 1, n), lambda i: (layer, 0, 0))
    weight = lambda rows, blk: pl.BlockSpec((rows, d), lambda i: (blk, 0),
                                            pipeline_mode=pl.Buffered(1))
    return pl.pallas_call(
        _out_proj_kernel,
        grid=(m // tm,),
        in_specs=[pl.BlockSpec((tm, ka), lambda i: (i, 0)),
                  pl.BlockSpec((tm, kr), lambda i: (i, 0)),
                  pl.BlockSpec((tm, d), lambda i: (i, 0)),
                  weight(ka, 0), weight(kr, 1),
                  gain(ka), gain(d), gain(d)],
        out_specs=[pl.BlockSpec((tm, d), lambda i: (i, 0)),
                   pl.BlockSpec((tm, d), lambda i: (i, 0))],
        out_shape=[jax.ShapeDtypeStruct((m, d), F32), jax.ShapeDtypeStruct((m, d), BF16)],
        compiler_params=pltpu.CompilerParams(
            dimension_semantics=("parallel",),
            vmem_limit_bytes=_vmem_limit(
                [((tm, ka), F32), ((tm, kr), BF16), ((tm, d), F32),
                 ((tm, d), F32), ((tm, d), BF16)],
                single=[((ka, d), BF16), ((kr, d), BF16)],
                temps=[((tm, d), F32)] * 3)),
        name="out_proj",
    )(ya, mr, h, w_out, w_out, g_attn, g_post, g_ffn)


def _ffn_kernel(u_ref, wg_ref, wu_ref, wd_ref, h_ref, gpost_ref, *rest, emit_next):
    if emit_next:
        gnext_ref, hout_ref, unext_ref = rest
    else:
        (hout_ref,) = rest
    acc_ref = hout_ref
    f = pl.program_id(1)

    @pl.when(f == 0)
    def _():
        acc_ref[...] = jnp.zeros_like(acc_ref)

    u = u_ref[...]
    g = jnp.dot(u, wg_ref[...], preferred_element_type=F32)
    up = jnp.dot(u, wu_ref[...], preferred_element_type=F32)
    act = (g * jax.nn.sigmoid(g) * up).astype(BF16)
    acc_ref[...] += jnp.dot(act, wd_ref[...], preferred_element_type=F32)

    @pl.when(f == pl.num_programs(1) - 1)
    def _():
        for r0 in range(0, acc_ref.shape[0], FFN_EPILOGUE_ROWS):
            rows = slice(r0, r0 + FFN_EPILOGUE_ROWS)
            hn = h_ref[rows, :] + _rms(acc_ref[rows, :], gpost_ref[...])
            hout_ref[rows, :] = hn
            if emit_next:
                unext_ref[rows, :] = _rms(hn, gnext_ref[...]).astype(unext_ref.dtype)


def _ffn(u, h, wg, wu, wd, g_post, g_next, layer, *, tm=1024, tf=512):
    m, d = h.shape
    dff = wg.shape[1]
    emit_next = g_next is not None
    once = lambda: pl.BlockSpec((tm, d), lambda i, f: (i, 0), pipeline_mode=pl.Buffered(1))
    operands = [u, wg, wu, wd, h, g_post]
    in_specs = [pl.BlockSpec((tm, d), lambda i, f: (i, 0)),
                pl.BlockSpec((d, tf), lambda i, f: (0, f)),
                pl.BlockSpec((d, tf), lambda i, f: (0, f)),
                pl.BlockSpec((tf, d), lambda i, f: (f, 0)),
                once(),
                pl.BlockSpec((None, 1, d), lambda i, f: (layer, 0, 0))]
    out_specs = [once()]
    out_shape = [jax.ShapeDtypeStruct((m, d), F32)]
    blocks = [((tm, d), BF16), ((d, tf), BF16), ((d, tf), BF16), ((tf, d), BF16)]
    single = [((tm, d), F32), ((tm, d), F32)]
    if emit_next:
        operands.append(g_next)
        in_specs.append(pl.BlockSpec((None, 1, d), lambda i, f: (layer + 1, 0, 0)))
        out_specs.append(once())
        out_shape.append(jax.ShapeDtypeStruct((m, d), BF16))
        single.append(((tm, d), BF16))
    return pl.pallas_call(
        functools.partial(_ffn_kernel, emit_next=emit_next),
        grid=(m // tm, dff // tf),
        in_specs=in_specs,
        out_specs=out_specs,
        out_shape=out_shape,
        compiler_params=pltpu.CompilerParams(
            dimension_semantics=("parallel", "arbitrary"),
            vmem_limit_bytes=_vmem_limit(
                blocks, single=single,
                temps=[((tm, tf), F32)] * 3 + [((FFN_EPILOGUE_ROWS, d), F32)] * 4)),
        name="swiglu_ffn",
    )(*operands)


def kernel(x, w_in, w_out, g_pre_mix, g_post_mix, g_pre_ffn, g_post_ffn, g_attn_grp,
           g_conv_grp, g_lru_grp, dw_conv_w, dw_conv_b, conv_ln_g, conv_ln_b, lru_conv_w,
           lru_conv_b, lru_w_a, lru_b_a, lru_w_i, lru_b_i, lru_lambda, w_gate, w_up, w_down):
    bsz, s_len, d = x.shape
    depth = w_in.shape[0]
    attn_w = g_attn_grp.shape[1]
    ch = g_conv_grp.shape[1]
    heads = attn_w // HEAD_DIM
    m = bsz * s_len
    assert w_in.shape[2] == 3 * attn_w + 4 * ch and lru_w_a.shape[1] == LRU_BLOCKS

    wai = jnp.concatenate([lru_w_a, lru_w_i], axis=-1).astype(BF16)

    row3 = lambda v: v[:, None, :]
    pv = jnp.concatenate(
        [row3(dw_conv_b), row3(conv_ln_g), row3(conv_ln_b), row3(g_conv_grp), row3(lru_conv_b),
         row3(lru_b_a), row3(lru_b_i), row3(lru_lambda), row3(g_lru_grp), lru_conv_w,
         jnp.zeros((depth, 3, ch), F32)], axis=1)
    dww = jnp.concatenate([dw_conv_w, jnp.zeros((depth, 1, ch), F32)], axis=1)
    g_pre_mix3, g_post_mix3 = row3(g_pre_mix), row3(g_post_mix)
    g_pre_ffn3, g_post_ffn3 = row3(g_pre_ffn), row3(g_post_ffn)
    g_attn3 = row3(g_attn_grp)

    jj = lax.broadcasted_iota(jnp.int32, (ATTN_BLK, ATTN_BLK), 0)
    ss = lax.broadcasted_iota(jnp.int32, (ATTN_BLK, ATTN_BLK), 1)
    tri = jnp.concatenate([(jj > ss).astype(BF16), jnp.ones((ATTN_BLK, ATTN_BLK), BF16)], axis=1)
    tri = jnp.concatenate([tri, tri], axis=0)

    tn = 1024
    h = x.reshape(m, d)
    u = _prenorm(h, g_pre_mix3, 0)
    for l in range(depth):
        (qkv,) = _in_proj(u, w_in, l, col_block0=0, n_out=3 * attn_w, out_dtype=BF16, tn=tn)
        rest, w_out_b = _in_proj(u, w_in, l, col_block0=3 * attn_w // tn, n_out=4 * ch,
                                 out_dtype=F32, cast_srcs=(w_out,), tn=tn)
        ya, w_gate_b, w_up_b = _attention(qkv.reshape(bsz, s_len, 3 * attn_w), tri, heads=heads,
                                          cast_srcs=(w_gate, w_up), cast_layer=l)
        mr, w_down_b = _mixers(rest.reshape(bsz, s_len, 4 * ch), dww, pv, wai, l,
                               cast_srcs=(w_down,))
        h, u2 = _out_proj(ya.reshape(m, attn_w), mr.reshape(m, 2 * ch), h, w_out_b,
                          g_attn3, g_post_mix3, g_pre_ffn3, l)
        if l == depth - 1:
            (h,) = _ffn(u2, h, w_gate_b, w_up_b, w_down_b, g_post_ffn3, None, l)
        else:
            h, u = _ffn(u2, h, w_gate_b, w_up_b, w_down_b, g_post_ffn3, g_pre_mix3, l)
    return h.reshape(bsz, s_len, d)
```

```python
import functools
from typing import Callable, NamedTuple

import jax
import jax.numpy as jnp
from jax import lax
from jax.experimental import pallas as pl
from jax.experimental.pallas import tpu as pltpu

F32 = jnp.float32
BF16 = jnp.bfloat16

EPS = 1e-6
HEAD_DIM = 128
DW_CONV_LEN = 31
LRU_CONV_LEN = 4
LRU_BLOCKS = 4
LRU_C = 8.0

V7X_VMEM_BYTES = 64 * 1024 * 1024
VMEM_CAP_BYTES = V7X_VMEM_BYTES - 8 * 1024 * 1024
SUBLANES = 8
LANES = 128

LOG_WEIGHT_FLOOR = -104.0

ATTN_BLK = LANES
ATTN_FIRST_BLOCKS = 3

CONV_HALO = 32
LRU_HALO = 8
CONV_ROWS = 64
OUT_PROJ_GROUPS = 2
FFN_TILE = 512


def _nbytes(shape, dtype):
    n = 1
    for s in shape:
        n *= s
    return n * jnp.dtype(dtype).itemsize


def _vmem_limit(blocks, scratch=(), temps=(), single=()):
    total = 2 * sum(_nbytes(s, d) for s, d in blocks)
    total += sum(_nbytes(s, d) for s, d in single)
    total += sum(_nbytes(s, d) for s, d in scratch)
    total += sum(_nbytes(s, d) for s, d in temps)
    assert total <= VMEM_CAP_BYTES, total
    return total


class _Cast(NamedTuple):
    src: jax.Array
    out_shape: tuple
    block: tuple
    in_index: Callable
    out_index: Callable


def _row_casts(srcs, layer, grid):
    steps = 1
    for g in grid:
        steps *= g

    def linear(*idx):
        step = idx[0]
        for g, i in zip(grid[1:], idx[1:]):
            step = step * g + i
        return step

    casts = []
    for src in srcs:
        _, rows, cols = src.shape
        block_rows = rows // steps
        assert block_rows * steps == rows and block_rows % (2 * SUBLANES) == 0
        casts.append(_Cast(src, (rows, cols), (block_rows, cols),
                           lambda *idx: (layer, linear(*idx), 0),
                           lambda *idx: (linear(*idx), 0)))
    return casts


def _with_casts(body, n_in, n_out, casts, pair_tile=None):
    n_cast = len(casts)
    n_cast_out = 1 if pair_tile else n_cast

    def kernel_fn(*refs):
        ins, refs = refs[:n_in], refs[n_in:]
        cast_in, refs = refs[:n_cast], refs[n_cast:]
        outs, refs = refs[:n_out], refs[n_out:]
        cast_out, scratch = refs[:n_cast_out], refs[n_cast_out:]
        body(*ins, *outs, *scratch)
        if pair_tile:
            (dst,), t = cast_out, pair_tile
            for f in range(dst.shape[0]):
                for half, src in enumerate(cast_in):
                    dst[f, :, half * t:(half + 1) * t] = src[:, f * t:(f + 1) * t].astype(dst.dtype)
        else:
            for src, dst in zip(cast_in, cast_out):
                dst[...] = src[...].astype(dst.dtype)

    in_specs = [pl.BlockSpec((None,) + c.block, c.in_index) for c in casts]
    vmem = [(c.block, F32) for c in casts] + [(c.block, BF16) for c in casts]
    if pair_tile:
        c = casts[0]
        rows, cols = c.out_shape
        tiles = cols // pair_tile
        out_block = (tiles, c.block[0], 2 * pair_tile)
        out_specs = [pl.BlockSpec(out_block, lambda *idx: (0,) + tuple(c.out_index(*idx)))]
        out_shapes = [jax.ShapeDtypeStruct((tiles, rows, 2 * pair_tile), BF16)]
    else:
        out_specs = [pl.BlockSpec(c.block, c.out_index) for c in casts]
        out_shapes = [jax.ShapeDtypeStruct(c.out_shape, BF16) for c in casts]
    return kernel_fn, in_specs, out_specs, out_shapes, [c.src for c in casts], vmem


def _rms(x, g):
    ms = jnp.mean(x * x, axis=-1, keepdims=True)
    return x * lax.rsqrt(ms + EPS) * g


def _softplus(z):
    return jnp.maximum(z, 0.0) + jnp.log(1.0 + jnp.exp(-jnp.abs(z)))


def _prenorm_kernel(x_ref, g_ref, o_ref):
    o_ref[...] = _rms(x_ref[...], g_ref[...]).astype(o_ref.dtype)


def _prenorm(h, g, layer, *, tm=512):
    m, d = h.shape
    return pl.pallas_call(
        _prenorm_kernel,
        grid=(m // tm,),
        in_specs=[pl.BlockSpec((tm, d), lambda i: (i, 0)),
                  pl.BlockSpec((None, 1, d), lambda i: (layer, 0, 0))],
        out_specs=pl.BlockSpec((tm, d), lambda i: (i, 0)),
        out_shape=jax.ShapeDtypeStruct((m, d), BF16),
        compiler_params=pltpu.CompilerParams(
            dimension_semantics=("parallel",),
            vmem_limit_bytes=_vmem_limit([((tm, d), F32), ((tm, d), BF16)],
                                         temps=[((tm, d), F32)] * 2)),
        name="prenorm",
    )(h, g)


def _in_proj_kernel(a_ref, w_ref, o_ref, wb_ref):
    @pl.when(pl.program_id(1) == 0)
    def _():
        wb_ref[...] = w_ref[...].astype(BF16)

    o_ref[...] = jnp.dot(a_ref[...], wb_ref[...],
                         preferred_element_type=F32).astype(o_ref.dtype)


def _in_proj(u, w, layer, *, col_block0, n_out, out_dtype, cast_srcs=(), tm=1024, tn=1024):
    m, k = u.shape
    grid = (n_out // tn, m // tm)
    body, c_in, c_out, c_shapes, c_ops, c_vmem = _with_casts(
        _in_proj_kernel, 2, 1, _row_casts(cast_srcs, layer, grid))
    return pl.pallas_call(
        body,
        grid=grid,
        in_specs=[pl.BlockSpec((tm, k), lambda j, i: (i, 0)),
                  pl.BlockSpec((None, k, tn), lambda j, i: (layer, 0, j + col_block0))] + c_in,
        out_specs=[pl.BlockSpec((tm, tn), lambda j, i: (i, j))] + c_out,
        out_shape=[jax.ShapeDtypeStruct((m, n_out), out_dtype)] + c_shapes,
        scratch_shapes=[pltpu.VMEM((k, tn), BF16)],
        compiler_params=pltpu.CompilerParams(
            dimension_semantics=("parallel", "arbitrary"),
            vmem_limit_bytes=_vmem_limit(
                [((tm, k), BF16), ((k, tn), F32), ((tm, tn), out_dtype)] + c_vmem,
                scratch=[((k, tn), BF16)], temps=[((tm, tn), F32)])),
        name="in_proj",
    )(u, w, *c_ops)


def _attn_kernel(q_ref, k_ref, v_ref, tri_ref, o_ref, carry_ref, *, units):
    tb = ATTN_BLK
    qi = pl.program_id(2)
    q0 = qi * (units * tb)
    tri = tri_ref[...]
    scale = HEAD_DIM ** -0.5
    tt = lax.broadcasted_iota(jnp.int32, (tb, tb), 0)
    ss = lax.broadcasted_iota(jnp.int32, (tb, tb), 1)
    below_diag = ss < tt

    def last_block(x, nblk, fn):
        head, tail = x[:, :(nblk - 1) * tb], fn(x[:, (nblk - 1) * tb:])
        return tail if nblk == 1 else jnp.concatenate([head, tail], axis=1)

    def sweep(starts, blocks, *, diag, carries=None, exists=None):
        rng = range(units)
        kbs = [k_ref[pl.ds(starts[r], blocks[r] * tb), :] for r in rng]
        vbs = [v_ref[pl.ds(starts[r], blocks[r] * tb), :] for r in rng]
        zs = [lax.dot_general(q_ref[r * tb:(r + 1) * tb, :], kbs[r], (((1,), (1,)), ((), ())),
                              preferred_element_type=F32) * scale for r in rng]
        drops, pieces = [], []
        for r in rng:
            nblk = blocks[r]
            drop = _softplus(zs[r])
            if diag:
                drop = last_block(drop, nblk, lambda x: jnp.where(below_diag, x, 0.0))
            if exists is not None:
                drop = jnp.where(exists[r], drop, 0.0)
            hi = drop.astype(BF16)
            lo = (drop - hi.astype(F32)).astype(BF16)
            drops.append(drop)
            pieces.append(jnp.concatenate(
                [jnp.concatenate([hi[:, j * tb:(j + 1) * tb], lo[:, j * tb:(j + 1) * tb]], axis=1)
                 for j in range(nblk)], axis=0))
        sums = [jnp.dot(pieces[r], tri, preferred_element_type=F32) for r in rng]
        ws, new_carries = [], []
        for r in rng:
            nblk = blocks[r]
            running = None if carries is None else carries[r]
            later = [None] * nblk
            for j in reversed(range(nblk)):
                local = sums[r][j * tb:(j + 1) * tb, :tb]
                total = sums[r][j * tb:(j + 1) * tb, tb:]
                later[j] = local if running is None else local + running
                running = total if running is None else running + total
            log_w = zs[r] - drops[r] - (later[0] if nblk == 1
                                        else jnp.concatenate(later, axis=1))
            if diag:
                log_w = last_block(log_w, nblk, lambda x: jnp.where(below_diag, x, -jnp.inf))
            w = jnp.exp(log_w)
            if exists is not None:
                w = jnp.where(exists[r], w, 0.0)
            ws.append(w.astype(BF16))
            new_carries.append(running)
        outs = [jnp.dot(ws[r], vbs[r], preferred_element_type=F32) for r in rng]
        for r in rng:
            rows = slice(r * tb, (r + 1) * tb)
            o_ref[rows, :] = outs[r] if carries is None else o_ref[rows, :] + outs[r]
            carry_ref[r] = new_carries[r]

    @pl.when(qi == 0)
    def _():
        blocks = [min(r + 1, ATTN_FIRST_BLOCKS) for r in range(units)]
        sweep([(r + 1 - n) * tb for r, n in enumerate(blocks)], blocks, diag=True)

    @pl.when(qi > 0)
    def _():
        sweep([pl.multiple_of(q0 + (r + 1 - ATTN_FIRST_BLOCKS) * tb, tb) for r in range(units)],
              [ATTN_FIRST_BLOCKS] * units, diag=True)

    def next_start(r, it):
        return q0 + (r - ATTN_FIRST_BLOCKS - it) * tb

    def pending_min(it):
        m = None
        for r in range(units):
            c = jnp.where(next_start(r, it) >= 0, carry_ref[r], jnp.inf)
            m = c if m is None else jnp.minimum(m, c)
        return jnp.min(m)

    def cond(c):
        _, cmin = c
        return cmin < -LOG_WEIGHT_FLOOR

    def body(c):
        it, _ = c
        starts = [next_start(r, it) for r in range(units)]
        sweep([pl.multiple_of(jnp.maximum(k_lo, 0), tb) for k_lo in starts], [1] * units,
              diag=False, carries=[carry_ref[r] for r in range(units)],
              exists=[k_lo >= 0 for k_lo in starts])
        return it + 1, pending_min(it + 1)

    lax.while_loop(cond, body, (jnp.int32(0), pending_min(0)))


def _attention(qkv, tri, *, heads, cast_srcs=(), cast_layer=0, pair_tile=None, units=16):
    b, s, _ = qkv.shape
    dh = HEAD_DIM
    tq = units * ATTN_BLK
    grid = (b, heads, s // tq)
    body, c_in, c_out, c_shapes, c_ops, c_vmem = _with_casts(
        functools.partial(_attn_kernel, units=units), 4, 1,
        _row_casts(cast_srcs, cast_layer, grid), pair_tile=pair_tile)
    return pl.pallas_call(
        body,
        grid=grid,
        in_specs=[pl.BlockSpec((None, tq, dh), lambda bi, h, i: (bi, i, h)),
                  pl.BlockSpec((None, s, dh), lambda bi, h, i: (bi, 0, heads + h)),
                  pl.BlockSpec((None, s, dh), lambda bi, h, i: (bi, 0, 2 * heads + h)),
                  pl.BlockSpec(tri.shape, lambda bi, h, i: (0, 0))] + c_in,
        out_specs=[pl.BlockSpec((None, tq, dh), lambda bi, h, i: (bi, i, h))] + c_out,
        out_shape=[jax.ShapeDtypeStruct((b, s, heads * dh), F32)] + c_shapes,
        scratch_shapes=[pltpu.VMEM((units, ATTN_BLK, ATTN_BLK), F32)],
        compiler_params=pltpu.CompilerParams(
            dimension_semantics=("parallel", "parallel", "parallel"),
            vmem_limit_bytes=_vmem_limit(
                [((tq, dh), BF16), ((s, dh), BF16), ((s, dh), BF16),
                 (tri.shape, BF16), ((tq, dh), F32)] + c_vmem,
                scratch=[((units, ATTN_BLK, ATTN_BLK), F32)],
                temps=[((ATTN_BLK, ATTN_FIRST_BLOCKS * ATTN_BLK), F32)] * 8 * units)),
        name="stickbreak_attn",
    )(qkv, qkv, qkv, tri, *c_ops)


def _mixer_kernel(cv_ref, cg_ref, rx_ref, ry_ref, dww_ref, pv_ref, wai_ref, o_ref,
                  ush, rxbuf, hcarry, wtile, *, ts, ch):
    s = pl.program_id(1)

    @pl.when(s == 0)
    def _():
        ush[0, 0:CONV_HALO, :] = jnp.zeros((CONV_HALO, ch), F32)
        rxbuf[0:LRU_HALO, :] = jnp.zeros((LRU_HALO, ch), F32)
        hcarry[...] = jnp.zeros_like(hcarry)
        for k in range(DW_CONV_LEN):
            wtile[k] = jnp.broadcast_to(dww_ref[k:k + 1, :], (SUBLANES, ch))

    ush[0, CONV_HALO:CONV_HALO + ts, :] = cv_ref[...] * jax.nn.sigmoid(cg_ref[...])
    u_all = ush[0]
    for p in range(1, SUBLANES):
        ush[p] = pltpu.roll(u_all, p, 0)
    base = CONV_HALO - (DW_CONV_LEN - 1)
    for rt in range(ts // CONV_ROWS):
        r0 = rt * CONV_ROWS
        acc = jnp.broadcast_to(pv_ref[0:1, :], (CONV_ROWS, ch))
        for k in range(DW_CONV_LEN):
            p = -(base + k) % SUBLANES
            lo = base + k + p + r0
            wk = jnp.concatenate([wtile[k]] * (CONV_ROWS // SUBLANES), axis=0)
            acc = acc + wk * ush[p, lo:lo + CONV_ROWS, :]
        mu = jnp.mean(acc, axis=-1, keepdims=True)
        xc = acc - mu
        var = jnp.mean(xc * xc, axis=-1, keepdims=True)
        y = xc * lax.rsqrt(var + EPS) * pv_ref[1:2, :] + pv_ref[2:3, :]
        y = y * jax.nn.sigmoid(y)
        o_ref[r0:r0 + CONV_ROWS, 0:ch] = _rms(y, pv_ref[3:4, :]).astype(o_ref.dtype)
    ush[0, 0:CONV_HALO, :] = ush[0, ts:ts + CONV_HALO, :]

    rxbuf[LRU_HALO:LRU_HALO + ts, :] = rx_ref[...]
    base = LRU_HALO - (LRU_CONV_LEN - 1)
    xr = jnp.broadcast_to(pv_ref[4:5, :], (ts, ch))
    for k in range(LRU_CONV_LEN):
        xr = xr + pv_ref[9 + k:10 + k, :] * rxbuf[base + k:base + k + ts, :]
    rxbuf[0:LRU_HALO, :] = rxbuf[ts:ts + LRU_HALO, :]
    xb = xr.astype(BF16)
    bd = ch // LRU_BLOCKS
    gates = [jnp.dot(xb[:, n * bd:(n + 1) * bd], wai_ref[n], preferred_element_type=F32)
             for n in range(LRU_BLOCKS)]
    gate_a = jnp.concatenate([g[:, :bd] for g in gates], axis=1) + pv_ref[5:6, :]
    gate_i = jnp.concatenate([g[:, bd:] for g in gates], axis=1) + pv_ref[6:7, :]
    r = jax.nn.sigmoid(gate_a)
    i = jax.nn.sigmoid(gate_i)
    log_a = -LRU_C * r * _softplus(-pv_ref[7:8, :])
    a = jnp.exp(log_a)
    b = jnp.sqrt(-jnp.tanh(log_a) * (a * a + 1.0)) * (i * xr)
    groups = ts // SUBLANES
    a = a.reshape(groups, SUBLANES, ch)
    b = b.reshape(groups, SUBLANES, ch)
    sub = lax.broadcasted_iota(jnp.int32, (groups, SUBLANES, ch), 1)
    d = 1
    while d < SUBLANES:
        a_prev = jnp.where(sub >= d, pltpu.roll(a, d, 1), 1.0)
        b_prev = jnp.where(sub >= d, pltpu.roll(b, d, 1), 0.0)
        b = a * b_prev + b
        a = a * a_prev
        d *= 2
    h_prev = hcarry[...]
    hs = []
    for gi in range(groups):
        hg = b[gi] + a[gi] * h_prev
        hs.append(hg)
        h_prev = jnp.broadcast_to(hg[SUBLANES - 1:SUBLANES, :], (SUBLANES, ch))
    hcarry[...] = h_prev
    h = jnp.concatenate(hs, axis=0)
    y = h * jax.nn.gelu(ry_ref[...])
    o_ref[:, ch:2 * ch] = _rms(y, pv_ref[8:9, :]).astype(o_ref.dtype)


def _mixers(rest, dww, pv, wai, layer, *, cast_srcs=(), ts=512):
    b, s, four_ch = rest.shape
    ch = four_ch // 4
    bd = ch // LRU_BLOCKS
    assert ts % CONV_ROWS == 0 and CONV_HALO % SUBLANES == 0 and CONV_HALO >= DW_CONV_LEN
    blk = lambda c: pl.BlockSpec((None, ts, ch), lambda bi, si: (bi, si, c))
    grid = (b, s // ts)
    body, c_in, c_out, c_shapes, c_ops, c_vmem = _with_casts(
        functools.partial(_mixer_kernel, ts=ts, ch=ch), 7, 1, _row_casts(cast_srcs, layer, grid))
    return pl.pallas_call(
        body,
        grid=grid,
        in_specs=[blk(0), blk(1), blk(2), blk(3),
                  pl.BlockSpec((None,) + dww.shape[1:], lambda bi, si: (layer, 0, 0)),
                  pl.BlockSpec((None,) + pv.shape[1:], lambda bi, si: (layer, 0, 0)),
                  pl.BlockSpec((None, LRU_BLOCKS, bd, 2 * bd),
                               lambda bi, si: (layer, 0, 0, 0))] + c_in,
        out_specs=[pl.BlockSpec((None, ts, 2 * ch), lambda bi, si: (bi, si, 0))] + c_out,
        out_shape=[jax.ShapeDtypeStruct((b, s, 2 * ch), BF16)] + c_shapes,
        scratch_shapes=[pltpu.VMEM((SUBLANES, CONV_HALO + ts, ch), F32),
                        pltpu.VMEM((LRU_HALO + ts, ch), F32),
                        pltpu.VMEM((SUBLANES, ch), F32),
                        pltpu.VMEM((DW_CONV_LEN, SUBLANES, ch), F32)],
        compiler_params=pltpu.CompilerParams(
            dimension_semantics=("parallel", "arbitrary"),
            vmem_limit_bytes=_vmem_limit(
                [((ts, ch), F32)] * 4 + [(dww.shape[1:], F32), (pv.shape[1:], F32),
                                         ((LRU_BLOCKS, bd, 2 * bd), BF16), ((ts, 2 * ch), BF16)]
                + c_vmem,
                scratch=[((SUBLANES, CONV_HALO + ts, ch), F32), ((LRU_HALO + ts, ch), F32),
                         ((DW_CONV_LEN, SUBLANES, ch), F32)],
                temps=[((ts, ch), F32)] * 16)),
        name="seq_mixers",
    )(rest, rest, rest, rest, dww, pv, wai, *c_ops)


def _out_proj_kernel(ya_ref, mr_ref, h_ref, w1_ref, w2_ref, ga_ref, gpost_ref, gffn_ref,
                     hout_ref, u_ref):
    tm = ya_ref.shape[0]
    groups = [slice(p * tm // OUT_PROJ_GROUPS, (p + 1) * tm // OUT_PROJ_GROUPS)
              for p in range(OUT_PROJ_GROUPS)]
    nas = [_rms(ya_ref[r, :], ga_ref[...]).astype(BF16) for r in groups]
    outs = [jnp.dot(na, w1_ref[...], preferred_element_type=F32)
            + jnp.dot(mr_ref[r, :], w2_ref[...], preferred_element_type=F32)
            for na, r in zip(nas, groups)]
    for o, r in zip(outs, groups):
        hn = h_ref[r, :] + _rms(o, gpost_ref[...])
        hout_ref[r, :] = hn
        u_ref[r, :] = _rms(hn, gffn_ref[...]).astype(u_ref.dtype)


def _out_proj(ya, mr, h, w_out, g_attn, g_post, g_ffn, layer, *, tm=512):
    m, d = h.shape
    ka = ya.shape[1]
    kr = mr.shape[1]
    assert ka == kr and w_out.shape == (ka + kr, d)
    gain = lambda n: pl.BlockSpec((None, 1, n), lambda i: (layer, 0, 0))
    weight = lambda rows, blk: pl.BlockSpec((rows, d), lambda i: (blk, 0),
                                            pipeline_mode=pl.Buffered(1))
    return pl.pallas_call(
        _out_proj_kernel,
        grid=(m // tm,),
        in_specs=[pl.BlockSpec((tm, ka), lambda i: (i, 0)),
                  pl.BlockSpec((tm, kr), lambda i: (i, 0)),
                  pl.BlockSpec((tm, d), lambda i: (i, 0)),
                  weight(ka, 0), weight(kr, 1),
                  gain(ka), gain(d), gain(d)],
        out_specs=[pl.BlockSpec((tm, d), lambda i: (i, 0)),
                   pl.BlockSpec((tm, d), lambda i: (i, 0))],
        out_shape=[jax.ShapeDtypeStruct((m, d), F32), jax.ShapeDtypeStruct((m, d), BF16)],
        compiler_params=pltpu.CompilerParams(
            dimension_semantics=("parallel",),
            vmem_limit_bytes=_vmem_limit(
                [((tm, ka), F32), ((tm, kr), BF16), ((tm, d), F32),
                 ((tm, d), F32), ((tm, d), BF16)],
                single=[((ka, d), BF16), ((kr, d), BF16)],
                temps=[((tm, d), F32)] * 3)),
        name="out_proj",
    )(ya, mr, h, w_out, w_out, g_attn, g_post, g_ffn)


def _ffn_kernel(u_ref, wgu_ref, wd_ref, h_ref, gpost_ref, *rest, emit_next):
    if emit_next:
        gnext_ref, hout_ref, unext_ref, acc_ref = rest
    else:
        hout_ref, acc_ref = rest
    f = pl.program_id(1)

    @pl.when(f == 0)
    def _():
        acc_ref[...] = jnp.zeros_like(acc_ref)

    tf = wd_ref.shape[0]
    gu = jnp.dot(u_ref[...], wgu_ref[...], preferred_element_type=F32)
    g, up = gu[:, :tf], gu[:, tf:]
    act = (g * jax.nn.sigmoid(g) * up).astype(BF16)
    acc_ref[...] += jnp.dot(act, wd_ref[...], preferred_element_type=F32)

    @pl.when(f == pl.num_programs(1) - 1)
    def _():
        hn = h_ref[...] + _rms(acc_ref[...], gpost_ref[...])
        hout_ref[...] = hn
        if emit_next:
            unext_ref[...] = _rms(hn, gnext_ref[...]).astype(unext_ref.dtype)


def _ffn(u, h, wgu, wd, g_post, g_next, layer, *, tm=512, tf=FFN_TILE):
    m, d = h.shape
    dff = wd.shape[0]
    assert wgu.shape == (dff // tf, d, 2 * tf)
    emit_next = g_next is not None
    operands = [u, wgu, wd, h, g_post]
    in_specs = [pl.BlockSpec((tm, d), lambda i, f: (i, 0)),
                pl.BlockSpec((None, d, 2 * tf), lambda i, f: (f, 0, 0)),
                pl.BlockSpec((tf, d), lambda i, f: (f, 0)),
                pl.BlockSpec((tm, d), lambda i, f: (i, 0)),
                pl.BlockSpec((None, 1, d), lambda i, f: (layer, 0, 0))]
    out_specs = [pl.BlockSpec((tm, d), lambda i, f: (i, 0))]
    out_shape = [jax.ShapeDtypeStruct((m, d), F32)]
    blocks = [((tm, d), BF16), ((d, 2 * tf), BF16), ((tf, d), BF16),
              ((tm, d), F32), ((tm, d), F32)]
    if emit_next:
        operands.append(g_next)
        in_specs.append(pl.BlockSpec((None, 1, d), lambda i, f: (layer + 1, 0, 0)))
        out_specs.append(pl.BlockSpec((tm, d), lambda i, f: (i, 0)))
        out_shape.append(jax.ShapeDtypeStruct((m, d), BF16))
        blocks.append(((tm, d), BF16))
    return pl.pallas_call(
        functools.partial(_ffn_kernel, emit_next=emit_next),
        grid=(m // tm, dff // tf),
        in_specs=in_specs,
        out_specs=out_specs,
        out_shape=out_shape,
        scratch_shapes=[pltpu.VMEM((tm, d), F32)],
        compiler_params=pltpu.CompilerParams(
            dimension_semantics=("parallel", "arbitrary"),
            vmem_limit_bytes=_vmem_limit(blocks, scratch=[((tm, d), F32)],
                                         temps=[((tm, tf), F32)] * 3 + [((tm, d), F32)] * 2)),
        name="swiglu_ffn",
    )(*operands)


def kernel(x, w_in, w_out, g_pre_mix, g_post_mix, g_pre_ffn, g_post_ffn, g_attn_grp,
           g_conv_grp, g_lru_grp, dw_conv_w, dw_conv_b, conv_ln_g, conv_ln_b, lru_conv_w,
           lru_conv_b, lru_w_a, lru_b_a, lru_w_i, lru_b_i, lru_lambda, w_gate, w_up, w_down):
    bsz, s_len, d = x.shape
    depth = w_in.shape[0]
    attn_w = g_attn_grp.shape[1]
    ch = g_conv_grp.shape[1]
    heads = attn_w // HEAD_DIM
    m = bsz * s_len
    assert w_in.shape[2] == 3 * attn_w + 4 * ch and lru_w_a.shape[1] == LRU_BLOCKS

    wai = jnp.concatenate([lru_w_a, lru_w_i], axis=-1).astype(BF16)

    row3 = lambda v: v[:, None, :]
    pv = jnp.concatenate(
        [row3(dw_conv_b), row3(conv_ln_g), row3(conv_ln_b), row3(g_conv_grp), row3(lru_conv_b),
         row3(lru_b_a), row3(lru_b_i), row3(lru_lambda), row3(g_lru_grp), lru_conv_w,
         jnp.zeros((depth, 3, ch), F32)], axis=1)
    dww = jnp.concatenate([dw_conv_w, jnp.zeros((depth, 1, ch), F32)], axis=1)
    g_pre_mix3, g_post_mix3 = row3(g_pre_mix), row3(g_post_mix)
    g_pre_ffn3, g_post_ffn3 = row3(g_pre_ffn), row3(g_post_ffn)
    g_attn3 = row3(g_attn_grp)

    jj = lax.broadcasted_iota(jnp.int32, (ATTN_BLK, ATTN_BLK), 0)
    ss = lax.broadcasted_iota(jnp.int32, (ATTN_BLK, ATTN_BLK), 1)
    tri = jnp.concatenate([(jj > ss).astype(BF16), jnp.ones((ATTN_BLK, ATTN_BLK), BF16)], axis=1)
    tri = jnp.concatenate([tri, tri], axis=0)

    tn = 1024
    h = x.reshape(m, d)
    u = _prenorm(h, g_pre_mix3, 0)
    for l in range(depth):
        (qkv,) = _in_proj(u, w_in, l, col_block0=0, n_out=3 * attn_w, out_dtype=BF16, tn=tn)
        rest, w_out_b = _in_proj(u, w_in, l, col_block0=3 * attn_w // tn, n_out=4 * ch,
                                 out_dtype=F32, cast_srcs=(w_out,), tn=tn)
        ya, w_gu_b = _attention(qkv.reshape(bsz, s_len, 3 * attn_w), tri, heads=heads,
                                cast_srcs=(w_gate, w_up), cast_layer=l, pair_tile=FFN_TILE)
        mr, w_down_b = _mixers(rest.reshape(bsz, s_len, 4 * ch), dww, pv, wai, l,
                               cast_srcs=(w_down,))
        h, u2 = _out_proj(ya.reshape(m, attn_w), mr.reshape(m, 2 * ch), h, w_out_b,
                          g_attn3, g_post_mix3, g_pre_ffn3, l)
        if l == depth - 1:
            (h,) = _ffn(u2, h, w_gu_b, w_down_b, g_post_ffn3, None, l)
        else:
            h, u = _ffn(u2, h, w_gu_b, w_down_b, g_post_ffn3, g_pre_mix3, l)
    return h.reshape(bsz, s_len, d)
```
